```python
import math
import jax
import jax.numpy as jnp
from jax import lax
import numpy as np

D_MODEL = 4096
BATCH = 16
SEQ = 256
DEPTH = 1
DEC_BATCH = 4
DEC_SEQ = 2048
PAST_LEN = 512

GRID_W = 64
D_MIX = D_MODEL
HG_WIDTH = D_MIX // 2
HG_DK = 128
HG_DV = 128
HG_HEADS = HG_WIDTH // HG_DK
GLA_WIDTH = D_MIX - HG_WIDTH
GLA_HEADS = 4
GLA_DV = GLA_WIDTH // GLA_HEADS
GLA_DK = GLA_DV // 2
GLA_KW = GLA_HEADS * GLA_DK
GLA_RANK = 16
GLA_LOGIT_NORM = 16.0
CHUNK = 32
N_EXPERTS = 256
TOP_K = 8
N_GROUP = 8
TOPK_GROUP = 4
D_EXPERT = 1024
ROUTED_SCALE = 2.5
MOE_BLOCK = 64
EPS = 1e-6
IN_SIZES = (HG_WIDTH, HG_WIDTH, HG_WIDTH, HG_WIDTH, HG_WIDTH, GLA_KW, GLA_KW, GLA_WIDTH, GLA_RANK, GLA_RANK, GLA_WIDTH)
D_IN = sum(IN_SIZES)
IN_SPLIT_POINTS = tuple(int(v) for v in np.cumsum(IN_SIZES)[:-1])

kernel_name = 'hgrn2_gla_moe_diffusion_step'


def rms_norm(x, w):
    xf = x.astype(jnp.float32)
    y = xf * lax.rsqrt(jnp.mean(xf * xf, axis=-1, keepdims=True) + EPS)
    return (y * w.astype(jnp.float32)).astype(x.dtype)


def grid_pos_embed(n_tokens):
    rows = n_tokens // GRID_W
    r, col = jnp.meshgrid(jnp.arange(rows, dtype=jnp.float32), jnp.arange(GRID_W, dtype=jnp.float32), indexing='ij')
    n_freq = D_MODEL // 4
    omega = 1.0 / (10000.0 ** (jnp.arange(n_freq, dtype=jnp.float32) / n_freq))
    def enc(p):
        ang = p.reshape(-1)[:, None] * omega[None, :]
        return jnp.concatenate([jnp.sin(ang), jnp.cos(ang)], axis=-1)
    return jnp.concatenate([enc(r), enc(col)], axis=-1)


def to_heads(t, n_heads):
    b, l, w = t.shape
    return t.reshape(b, l, n_heads, w // n_heads).transpose(0, 2, 1, 3)


def chunk_recurrence(q, k, v, log_a, s0, scale):
    b, h, l, dk = q.shape
    dv = v.shape[-1]
    n = l // CHUNK
    f32 = jnp.float32
    def to_chunks(t):
        return t.astype(f32).reshape(b, h, n, CHUNK, t.shape[-1]).transpose(2, 0, 1, 3, 4)
    causal = jnp.tril(jnp.ones((CHUNK, CHUNK), dtype=bool))[None, None, :, :, None]
    def step(s, inp):
        qc, kc, vc, ac = inp
        qc = qc * scale
        cum = jnp.cumsum(ac, axis=2)
        o_inter = jnp.einsum('bhcd,bhde->bhce', qc * jnp.exp(cum), s)
        diff = cum[:, :, :, None, :] - cum[:, :, None, :, :]
        dec = jnp.exp(jnp.where(causal, diff, -jnp.inf))
        att = jnp.einsum('bhid,bhjd,bhijd->bhij', qc, kc, dec)
        o = o_inter + jnp.einsum('bhij,bhje->bhie', att, vc)
        last = cum[:, :, -1, :]
        s_new = s * jnp.exp(last)[..., None] + jnp.einsum('bhcd,bhce->bhde', kc * jnp.exp(last[:, :, None, :] - cum), vc)
        return s_new, o
    s_fin, o = lax.scan(step, s0.astype(f32), (to_chunks(q), to_chunks(k), to_chunks(v), to_chunks(log_a)))
    return o.transpose(1, 2, 0, 3, 4).reshape(b, h, l, dv), s_fin


def bidirectional(q, k_f, k_b, v, la_f, la_b, s0_f, s0_b, scale):
    o_f, s_f = chunk_recurrence(q, k_f, v, la_f, s0_f, scale)
    rev = lambda t: jnp.flip(t, axis=2)
    o_b, s_b = chunk_recurrence(rev(q), rev(k_b), rev(v), rev(la_b), s0_b, scale)
    return o_f + rev(o_b), s_f, s_b


def gated_head_norm(o, w, g):
    o = o.transpose(0, 2, 1, 3)
    o = o * lax.rsqrt(jnp.mean(o * o, axis=-1, keepdims=True) + EPS) * w.astype(jnp.float32)
    b, l, h, dv = o.shape
    gate = jax.nn.silu(g.astype(jnp.float32)).reshape(b, l, h, dv)
    return (o * gate).reshape(b, l, h * dv)


def token_mixer(h, s0_hf, s0_hb, s0_gf, s0_gb, lb_f, lb_b, w_in, gk_up_f, gk_bias_f, gk_up_b, gk_bias_b, hg_norm_w, gla_norm_w, w_out):
    f32 = jnp.float32
    proj = jnp.einsum('bld,dn->bln', h, w_in)
    hq, hf_f, hf_b, hi, hg, gq, gk, gv, gl_f, gl_b, gg = jnp.split(proj, IN_SPLIT_POINTS, axis=-1)
    f_f = lb_f + (1.0 - lb_f) * jax.nn.sigmoid(hf_f.astype(f32))
    f_b = lb_b + (1.0 - lb_b) * jax.nn.sigmoid(hf_b.astype(f32))
    o_hg, st_hf, st_hb = bidirectional(
        to_heads(hq, HG_HEADS), to_heads(1.0 - f_f, HG_HEADS), to_heads(1.0 - f_b, HG_HEADS),
        to_heads(hi, HG_HEADS), to_heads(jnp.log(f_f), HG_HEADS), to_heads(jnp.log(f_b), HG_HEADS),
        s0_hf, s0_hb, HG_DK ** -0.5)
    out_hg = gated_head_norm(o_hg, hg_norm_w, hg)
    la_f = jax.nn.log_sigmoid(jnp.einsum('blr,rk->blk', gl_f.astype(f32), gk_up_f.astype(f32)) + gk_bias_f.astype(f32)) / GLA_LOGIT_NORM
    la_b = jax.nn.log_sigmoid(jnp.einsum('blr,rk->blk', gl_b.astype(f32), gk_up_b.astype(f32)) + gk_bias_b.astype(f32)) / GLA_LOGIT_NORM
    gk_h = to_heads(gk, GLA_HEADS)
    o_gla, st_gf, st_gb = bidirectional(
        to_heads(gq, GLA_HEADS), gk_h, gk_h, to_heads(gv, GLA_HEADS),
        to_heads(la_f, GLA_HEADS), to_heads(la_b, GLA_HEADS), s0_gf, s0_gb, GLA_DK ** -0.5)
    out_gla = gated_head_norm(o_gla, gla_norm_w, gg)
    mixed = jnp.concatenate([out_hg, out_gla], axis=-1).astype(h.dtype)
    return jnp.einsum('bln,nd->bld', mixed, w_out), st_hf, st_hb, st_gf, st_gb


def swiglu(x, wg, wu, wd):
    return jnp.dot(jax.nn.silu(jnp.dot(x, wg)) * jnp.dot(x, wu), wd)


def moe_ffn(h, router_w, router_bias, exp_w_gate, exp_w_up, exp_w_down, sh_w_gate, sh_w_up, sh_w_down):
    f32 = jnp.float32
    x = h.reshape(-1, D_MODEL)
    t = x.shape[0]
    scores = jax.nn.sigmoid(jnp.dot(x.astype(f32), router_w.astype(f32)))
    biased = scores + router_bias.astype(f32)
    grp_score = lax.top_k(biased.reshape(t, N_GROUP, N_EXPERTS // N_GROUP), 2)[0].sum(-1)
    _, gidx = lax.top_k(grp_score, TOPK_GROUP)
    gmask = jnp.any(gidx[:, :, None] == jnp.arange(N_GROUP)[None, None, :], axis=1)
    emask = jnp.repeat(gmask, N_EXPERTS // N_GROUP, axis=1)
    _, eidx = lax.top_k(jnp.where(emask, biased, -jnp.inf), TOP_K)
    wts = jnp.take_along_axis(scores, eidx, axis=1)
    wts = wts / jnp.sum(wts, axis=-1, keepdims=True) * ROUTED_SCALE
    n_assign = t * TOP_K
    flat_e = eidx.reshape(n_assign)
    order = jnp.argsort(flat_e)
    sorted_e = flat_e[order]
    counts = jnp.bincount(flat_e, length=N_EXPERTS)
    padded = (counts + MOE_BLOCK - 1) // MOE_BLOCK * MOE_BLOCK
    pad_end = jnp.cumsum(padded)
    pad_start = pad_end - padded
    start = jnp.cumsum(counts) - counts
    slot = pad_start[sorted_e] + jnp.arange(n_assign) - start[sorted_e]
    n_blocks = (n_assign + N_EXPERTS * (MOE_BLOCK - 1) + MOE_BLOCK - 1) // MOE_BLOCK
    n_slots = n_blocks * MOE_BLOCK
    slot_tok = jnp.full((n_slots,), t, jnp.int32).at[slot].set((order // TOP_K).astype(jnp.int32))
    slot_w = jnp.zeros((n_slots,), f32).at[slot].set(wts.reshape(n_assign)[order])
    block_e = jnp.minimum(jnp.searchsorted(pad_end, jnp.arange(n_blocks) * MOE_BLOCK, side='right'), N_EXPERTS - 1)
    x_pad = jnp.concatenate([x, jnp.zeros((1, D_MODEL), x.dtype)], axis=0)
    def body(acc, blk):
        tok, wt, e = blk
        y = swiglu(x_pad[tok], exp_w_gate[e], exp_w_up[e], exp_w_down[e])
        return acc.at[tok].add(y.astype(f32) * wt[:, None]), None
    acc, _ = lax.scan(body, jnp.zeros((t + 1, D_MODEL), f32),
                      (slot_tok.reshape(n_blocks, MOE_BLOCK), slot_w.reshape(n_blocks, MOE_BLOCK), block_e))
    out = acc[:t] + swiglu(x, sh_w_gate, sh_w_up, sh_w_down).astype(f32)
    return out.reshape(h.shape).astype(h.dtype)


def trunk_layer(x, mod, s0_hf, s0_hb, s0_gf, s0_gb, norm1_w, norm2_w, lb_f, lb_b, w_in, gk_up_f, gk_bias_f, gk_up_b, gk_bias_b,
                hg_norm_w, gla_norm_w, w_out, router_w, router_bias, exp_w_gate, exp_w_up, exp_w_down, sh_w_gate, sh_w_up, sh_w_down):
    f32 = jnp.float32
    shift1, scale1, gate1, shift2, scale2, gate2 = jnp.split(mod, 6, axis=-1)
    h = (rms_norm(x, norm1_w).astype(f32) * (1.0 + scale1) + shift1).astype(x.dtype)
    mix, st_hf, st_hb, st_gf, st_gb = token_mixer(h, s0_hf, s0_hb, s0_gf, s0_gb, lb_f, lb_b, w_in, gk_up_f, gk_bias_f,
                                                  gk_up_b, gk_bias_b, hg_norm_w, gla_norm_w, w_out)
    x = (x.astype(f32) + gate1 * mix.astype(f32)).astype(x.dtype)
    h = (rms_norm(x, norm2_w).astype(f32) * (1.0 + scale2) + shift2).astype(x.dtype)
    ff = moe_ffn(h, router_w, router_bias, exp_w_gate, exp_w_up, exp_w_down, sh_w_gate, sh_w_up, sh_w_down)
    x = (x.astype(f32) + gate2 * ff.astype(f32)).astype(x.dtype)
    return x, st_hf, st_hb, st_gf, st_gb


def setup_inputs(seed: int = 0) -> dict:
    key = jax.random.key(seed)
    ks = jax.random.split(key, 32)
    f32 = jnp.float32
    def nrm(k, shape, std):
        return std * jax.random.normal(k, shape, f32)
    def uni(k, shape, std):
        a = std * math.sqrt(3.0)
        return jax.random.uniform(k, shape, f32, -a, a)
    sd = D_MODEL ** -0.5
    return {
        'x_prompt': nrm(ks[0], (BATCH, SEQ, D_MODEL), 1.0),
        'x_sample': nrm(ks[1], (DEC_BATCH, DEC_SEQ, D_MODEL), 1.0),
        'state_hgrn_fwd': nrm(ks[2], (DEC_BATCH, DEPTH, HG_HEADS, HG_DK, HG_DV), 0.5),
        'state_hgrn_bwd': nrm(ks[3], (DEC_BATCH, DEPTH, HG_HEADS, HG_DK, HG_DV), 0.5),
        'state_gla_fwd': nrm(ks[4], (DEC_BATCH, DEPTH, GLA_HEADS, GLA_DK, GLA_DV), 0.5),
        'state_gla_bwd': nrm(ks[5], (DEC_BATCH, DEPTH, GLA_HEADS, GLA_DK, GLA_DV), 0.5),
        'c': nrm(ks[6], (DEC_BATCH, D_MODEL), 1.0),
        'c_ctx': nrm(ks[7], (D_MODEL,), 1.0),
        'norm1_w': 1.0 + nrm(ks[8], (DEPTH, D_MODEL), 0.02),
        'norm2_w': 1.0 + nrm(ks[9], (DEPTH, D_MODEL), 0.02),
        'ada_w': nrm(ks[10], (DEPTH, D_MODEL, 6 * D_MODEL), 0.5 * sd),
        'ada_b': nrm(ks[11], (DEPTH, 6 * D_MODEL), 0.02),
        'w_in': nrm(ks[12], (DEPTH, D_MODEL, D_IN), sd),
        'hg_lb_logits': nrm(ks[13], (2, DEPTH + 1, HG_WIDTH), 0.5),
        'gla_gk_up_fwd': nrm(ks[14], (DEPTH, GLA_RANK, GLA_KW), GLA_RANK ** -0.5),
        'gla_gk_bias_fwd': nrm(ks[15], (DEPTH, GLA_KW), 0.1),
        'gla_gk_up_bwd': nrm(ks[16], (DEPTH, GLA_RANK, GLA_KW), GLA_RANK ** -0.5),
        'gla_gk_bias_bwd': nrm(ks[17], (DEPTH, GLA_KW), 0.1),
        'hg_norm_w': 1.0 + nrm(ks[18], (DEPTH, HG_DV), 0.02),
        'gla_norm_w': 1.0 + nrm(ks[19], (DEPTH, GLA_DV), 0.02),
        'w_out': nrm(ks[20], (DEPTH, D_MIX, D_MODEL), D_MIX ** -0.5),
        'router_w': nrm(ks[21], (DEPTH, D_MODEL, N_EXPERTS), sd),
        'router_bias': nrm(ks[22], (DEPTH, N_EXPERTS), 0.01),
        'exp_w_gate': uni(ks[23], (DEPTH, N_EXPERTS, D_MODEL, D_EXPERT), sd),
        'exp_w_up': uni(ks[24], (DEPTH, N_EXPERTS, D_MODEL, D_EXPERT), sd),
        'exp_w_down': uni(ks[25], (DEPTH, N_EXPERTS, D_EXPERT, D_MODEL), D_EXPERT ** -0.5),
        'sh_w_gate': nrm(ks[26], (DEPTH, D_MODEL, D_EXPERT), sd),
        'sh_w_up': nrm(ks[27], (DEPTH, D_MODEL, D_EXPERT), sd),
        'sh_w_down': nrm(ks[28], (DEPTH, D_EXPERT, D_MODEL), D_EXPERT ** -0.5),
        'final_norm_w': 1.0 + nrm(ks[29], (D_MODEL,), 0.02),
    }


def reference(x_prompt, x_sample, state_hgrn_fwd, state_hgrn_bwd, state_gla_fwd, state_gla_bwd, c, c_ctx,
              norm1_w, norm2_w, ada_w, ada_b, w_in, hg_lb_logits, gla_gk_up_fwd, gla_gk_bias_fwd, gla_gk_up_bwd,
              gla_gk_bias_bwd, hg_norm_w, gla_norm_w, w_out, router_w, router_bias, exp_w_gate, exp_w_up, exp_w_down,
              sh_w_gate, sh_w_up, sh_w_down, final_norm_w):
    f32 = jnp.float32
    n_prompt = x_prompt.shape[0]
    lb_all = jnp.cumsum(jax.nn.softmax(hg_lb_logits.astype(f32), axis=1), axis=1)
    xp = x_prompt
    xs = (x_sample.astype(f32) + grid_pos_embed(x_sample.shape[1])[None]).astype(x_sample.dtype)
    z_h = jnp.zeros((n_prompt, HG_HEADS, HG_DK, HG_DV), f32)
    z_g = jnp.zeros((n_prompt, GLA_HEADS, GLA_DK, GLA_DV), f32)
    new_hf, new_hb, new_gf, new_gb = [], [], [], []
    for l in range(DEPTH):
        mod_ctx = (jnp.dot(jax.nn.silu(c_ctx.astype(f32)), ada_w[l].astype(f32)) + ada_b[l].astype(f32))[None, None, :]
        mod_lat = (jnp.dot(jax.nn.silu(c.astype(f32)), ada_w[l].astype(f32)) + ada_b[l].astype(f32))[:, None, :]
        shared = (norm1_w[l], norm2_w[l], lb_all[0, l], lb_all[1, l], w_in[l], gla_gk_up_fwd[l], gla_gk_bias_fwd[l],
                  gla_gk_up_bwd[l], gla_gk_bias_bwd[l], hg_norm_w[l], gla_norm_w[l], w_out[l], router_w[l],
                  router_bias[l], exp_w_gate[l], exp_w_up[l], exp_w_down[l], sh_w_gate[l], sh_w_up[l], sh_w_down[l])
        xp, s_hf, s_hb, s_gf, s_gb = trunk_layer(xp, mod_ctx, z_h, z_h, z_g, z_g, *shared)
        new_hf.append(s_hf)
        new_hb.append(s_hb)
        new_gf.append(s_gf)
        new_gb.append(s_gb)
        xs = trunk_layer(xs, mod_lat, state_hgrn_fwd[:, l], state_hgrn_bwd[:, l], state_gla_fwd[:, l],
                         state_gla_bwd[:, l], *shared)[0]
    y_prompt = rms_norm(xp, final_norm_w)
    y_sample = rms_norm(xs, final_norm_w)
    return (y_prompt, y_sample, jnp.stack(new_hf, axis=1), jnp.stack(new_hb, axis=1), jnp.stack(new_gf, axis=1), jnp.stack(new_gb, axis=1))
```

```python
import functools
import math

import numpy as np
import jax
import jax.numpy as jnp
from jax import lax
from jax.experimental import pallas as pl
from jax.experimental.pallas import tpu as pltpu

F32 = jnp.float32
BF16 = jnp.bfloat16

GRID_W = 64
TOP_K = 8
N_GROUP = 8
TOPK_GROUP = 4
ROUTED_SCALE = 2.5
GLA_LOGIT_NORM = 16.0
EPS = 1e-6

V7X_VMEM_LIMIT_BYTES = 56 * 1024 * 1024
LANES = 128
ROW_ALIGN = 16
REC_CHUNK = 128
MOE_UNIT_ROWS = 512
MOE_ROW_BLOCK = 256
MOE_F_TILE = 256


def _cparams(*sem):
    return pltpu.CompilerParams(dimension_semantics=sem, vmem_limit_bytes=V7X_VMEM_LIMIT_BYTES)


def _tile(n, preferred):
    t = min(preferred, n)
    while n % t:
        t //= 2
    return t


def _dot(a, b):
    return jnp.dot(a, b, preferred_element_type=F32)


def _dot_nt(a, b):
    return lax.dot_general(a, b, (((1,), (1,)), ((), ())), preferred_element_type=F32)


def _dot_tn(a, b):
    return lax.dot_general(a, b, (((0,), (0,)), ((), ())), preferred_element_type=F32)


def _silu(x):
    return x * jax.nn.sigmoid(x)


def _split_hi_lo(x):
    hi = x.astype(BF16)
    return hi, (x - hi.astype(F32)).astype(BF16)


def _ada_kernel(c_ref, w_ref, b_ref, o_ref):
    a = _silu(c_ref[...]).astype(BF16)
    o_ref[...] = _dot(a, w_ref[...].astype(BF16)) + b_ref[...]


def _ada(c_rows, ada_w, ada_b):
    d, n = ada_w.shape
    rows = c_rows.shape[0]
    tn = min(512, n)
    return pl.pallas_call(
        _ada_kernel,
        grid=(n // tn,),
        in_specs=[pl.BlockSpec((rows, d), lambda j: (0, 0)),
                  pl.BlockSpec((d, tn), lambda j: (0, j)),
                  pl.BlockSpec((1, tn), lambda j: (0, j))],
        out_specs=pl.BlockSpec((rows, tn), lambda j: (0, j)),
        out_shape=jax.ShapeDtypeStruct((rows, n), F32),
        compiler_params=_cparams("arbitrary"),
        name="ada_mod",
    )(c_rows, ada_w, ada_b)


def _modulated_norm(x, w, shift, scale):
    y = x * lax.rsqrt(jnp.mean(x * x, axis=-1, keepdims=True) + EPS) * w
    return y * (1.0 + scale) + shift


def _norm1_kernel(xp_ref, xs_ref, pos_ref, mod_ref, w_ref, x0_ref, h_ref, *, ctx_blocks):
    i = pl.program_id(0)

    def finish(x):
        x0_ref[...] = x
        h_ref[...] = _modulated_norm(x, w_ref[...], mod_ref[0, 0:1, :], mod_ref[0, 1:2, :]).astype(BF16)

    @pl.when(i < ctx_blocks)
    def _():
        finish(xp_ref[...])

    @pl.when(i >= ctx_blocks)
    def _():
        finish(xs_ref[...] + pos_ref[...])


def _mod_row_map(ctx_blocks, blocks_per_seq, ctx_row):
    return lambda i, *_: (jnp.where(i < ctx_blocks, ctx_row, (i - ctx_blocks) // blocks_per_seq), 0, 0)


def _norm1(xp, xs, pos, mod3, w, tm, lat_seq):
    tp, d = xp.shape
    ts = xs.shape[0]
    t = tp + ts
    cb = tp // tm
    bps = lat_seq // tm
    ctx_row = ts // lat_seq
    return pl.pallas_call(
        functools.partial(_norm1_kernel, ctx_blocks=cb),
        grid=(t // tm,),
        in_specs=[pl.BlockSpec((tm, d), lambda i: (jnp.minimum(i, cb - 1), 0)),
                  pl.BlockSpec((tm, d), lambda i: (jnp.maximum(i - cb, 0), 0)),
                  pl.BlockSpec((tm, d), lambda i: (jnp.maximum(i - cb, 0) % bps, 0)),
                  pl.BlockSpec((1, 6, d), _mod_row_map(cb, bps, ctx_row)),
                  pl.BlockSpec((1, d), lambda i: (0, 0))],
        out_specs=[pl.BlockSpec((tm, d), lambda i: (i, 0)),
                   pl.BlockSpec((tm, d), lambda i: (i, 0))],
        out_shape=[jax.ShapeDtypeStruct((t, d), F32), jax.ShapeDtypeStruct((t, d), BF16)],
        compiler_params=_cparams("arbitrary"),
        name="norm1",
    )(xp, xs, pos, mod3, w)


def _norm2_kernel(x0_ref, mc_ref, ml_ref, mod_ref, w_ref, x1_ref, h_ref, *, ctx_blocks):
    i = pl.program_id(0)

    def finish(mix):
        x1 = x0_ref[...] + mod_ref[0, 2:3, :] * mix
        x1_ref[...] = x1
        h_ref[...] = _modulated_norm(x1, w_ref[...], mod_ref[0, 3:4, :], mod_ref[0, 4:5, :])

    @pl.when(i < ctx_blocks)
    def _():
        finish(mc_ref[...])

    @pl.when(i >= ctx_blocks)
    def _():
        finish(ml_ref[...])


def _norm2(x0, mix_ctx, mix_lat, mod3, w, tm, lat_seq):
    t, d = x0.shape
    tp = mix_ctx.shape[0]
    cb = tp // tm
    bps = lat_seq // tm
    ctx_row = (t - tp) // lat_seq
    return pl.pallas_call(
        functools.partial(_norm2_kernel, ctx_blocks=cb),
        grid=(t // tm,),
        in_specs=[pl.BlockSpec((tm, d), lambda i: (i, 0)),
                  pl.BlockSpec((tm, d), lambda i: (jnp.minimum(i, cb - 1), 0)),
                  pl.BlockSpec((tm, d), lambda i: (jnp.maximum(i - cb, 0), 0)),
                  pl.BlockSpec((1, 6, d), _mod_row_map(cb, bps, ctx_row)),
                  pl.BlockSpec((1, d), lambda i: (0, 0))],
        out_specs=[pl.BlockSpec((tm, d), lambda i: (i, 0)),
                   pl.BlockSpec((tm, d), lambda i: (i, 0))],
        out_shape=[jax.ShapeDtypeStruct((t, d), F32), jax.ShapeDtypeStruct((t, d), F32)],
        compiler_params=_cparams("arbitrary"),
        name="norm2",
    )(x0, mix_ctx, mix_lat, mod3, w)


def _mm_kernel(a_ref, b_ref, o_ref):
    o_ref[...] = _dot(a_ref[...], b_ref[...])


def _matmul(a, b, tm, tn, name):
    m, k = a.shape
    n = b.shape[1]
    tm, tn = _tile(m, tm), _tile(n, tn)
    return pl.pallas_call(
        _mm_kernel,
        grid=(n // tn, m // tm),
        in_specs=[pl.BlockSpec((tm, k), lambda j, i: (i, 0)),
                  pl.BlockSpec((k, tn), lambda j, i: (0, j))],
        out_specs=pl.BlockSpec((tm, tn), lambda j, i: (i, j)),
        out_shape=jax.ShapeDtypeStruct((m, n), F32),
        compiler_params=_cparams("arbitrary", "arbitrary"),
        name=name,
    )(a, b)


def _mm2_kernel(a1_ref, b1_ref, a2_ref, b2_ref, o_ref):
    o_ref[...] = _dot(a1_ref[...], b1_ref[...]) + _dot(a2_ref[...], b2_ref[...])


def _matmul2(a1, b1, a2, b2, tm, tn, name):
    m, k1 = a1.shape
    k2 = a2.shape[1]
    n = b1.shape[1]
    tm, tn = _tile(m, tm), _tile(n, tn)
    return pl.pallas_call(
        _mm2_kernel,
        grid=(n // tn, m // tm),
        in_specs=[pl.BlockSpec((tm, k1), lambda j, i: (i, 0)),
                  pl.BlockSpec((k1, tn), lambda j, i: (0, j)),
                  pl.BlockSpec((tm, k2), lambda j, i: (i, 0)),
                  pl.BlockSpec((k2, tn), lambda j, i: (0, j))],
        out_specs=pl.BlockSpec((tm, tn), lambda j, i: (i, j)),
        out_shape=jax.ShapeDtypeStruct((m, n), F32),
        compiler_params=_cparams("arbitrary", "arbitrary"),
        name=name,
    )(a1, b1, a2, b2)


def _level_constants(c):
    nlev = int(math.log2(c))
    i = np.arange(c)[:, None]
    t = np.arange(c)[None, :]
    groups = [t <= i, t > i]
    masks = [i == t]
    for lev in range(nlev):
        s = 1 << lev
        base = (i // (2 * s)) * (2 * s)
        pivot = base + s - 1
        right = i >= base + s
        groups.append(np.where(right, (t > pivot) & (t <= i), (t > i) & (t <= pivot)))
        same = (i // (2 * s)) == (t // (2 * s))
        masks.append(same & right & ((t % (2 * s)) < s))
    bmat_f = np.concatenate([g.astype(np.float32) for g in groups], axis=0)
    bmat_b = np.concatenate([g[::-1, ::-1].astype(np.float32) for g in groups], axis=0)
    mask_f = np.stack([m.astype(np.float32) for m in masks])
    mask_b = np.stack([m[::-1, ::-1].astype(np.float32) for m in masks])
    return (jnp.asarray(bmat_f, BF16), jnp.asarray(bmat_b, BF16),
            jnp.asarray(mask_f, F32), jnp.asarray(mask_b, F32), nlev)


def _chunk_step(q_s, k, v_bf, nl, st_ref, bmat_ref, mask_ref, c, nlev, total_row):
    n1 = nl.astype(BF16)
    r1 = nl - n1.astype(F32)
    n2 = r1.astype(BF16)
    n3 = (r1 - n2.astype(F32)).astype(BF16)
    bm = bmat_ref[...]
    dec = _dot(bm, n1) + _dot(bm, n2) + _dot(bm, n3)
    cum = dec[0:c]
    tail = dec[c:2 * c]
    st = st_ref[...]
    o = _dot_nt((q_s * jnp.exp(-cum)).astype(BF16), st.astype(BF16))
    att = mask_ref[0] * _dot_nt(q_s.astype(BF16), k.astype(BF16))
    for lev in range(nlev):
        e = jnp.exp(-dec[(2 + lev) * c:(3 + lev) * c])
        att = att + mask_ref[1 + lev] * _dot_nt((q_s * e).astype(BF16), (k * e).astype(BF16))
    o = o + _dot(att.astype(BF16), v_bf)
    k_hat = (k * jnp.exp(-tail)).astype(BF16)
    total = cum[total_row:total_row + 1]
    st_ref[...] = st * jnp.exp(-total) + _dot_tn(v_bf, k_hat)
    return o


def _head_norm_out(of_scr, ob_scr, g_ref, nw_ref, out_ref):
    o = of_scr[...] + ob_scr[...]
    y = o * lax.rsqrt(jnp.mean(o * o, axis=-1, keepdims=True) + EPS) * nw_ref[...]
    out_ref[...] = (y * _silu(g_ref[...])).astype(BF16)


def _hgrn_kernel(*refs, c, nchunks, nlev, scale, has_s0):
    (q_ref, ff_ref, fb_ref, i_ref, g_ref, lbf_ref, lbb_ref, nw_ref,
     bmf_ref, bmb_ref, mkf_ref, mkb_ref) = refs[:12]
    refs = refs[12:]
    if has_s0:
        s0f_ref, s0b_ref = refs[:2]
        refs = refs[2:]
    out_ref, sf_ref, sb_ref, of_scr, ob_scr, stf_scr, stb_scr = refs

    if has_s0:
        stf_scr[...] = s0f_ref[...].T
        stb_scr[...] = s0b_ref[...].T
    else:
        stf_scr[...] = jnp.zeros_like(stf_scr)
        stb_scr[...] = jnp.zeros_like(stb_scr)

    def one_direction(chunk, gate_ref, lb_ref, st_scr, bm_ref, mk_ref, o_scr, total_row):
        rows = pl.ds(pl.multiple_of(chunk * c, c), c)
        lb = lb_ref[...]
        f = lb + (1.0 - lb) * jax.nn.sigmoid(gate_ref[rows, :])
        o_scr[rows, :] = _chunk_step(q_ref[rows, :] * scale, 1.0 - f, i_ref[rows, :].astype(BF16),
                                     -jnp.log(f), st_scr, bm_ref, mk_ref, c, nlev, total_row)

    def body(ci, carry):
        one_direction(ci, ff_ref, lbf_ref, stf_scr, bmf_ref, mkf_ref, of_scr, c - 1)
        one_direction(nchunks - 1 - ci, fb_ref, lbb_ref, stb_scr, bmb_ref, mkb_ref, ob_scr, 0)
        return carry

    lax.fori_loop(0, nchunks, body, 0)
    _head_norm_out(of_scr, ob_scr, g_ref, nw_ref, out_ref)
    sf_ref[...] = stf_scr[...].T
    sb_ref[...] = stb_scr[...].T


def _hgrn(proj, lb_f, lb_b, norm_w, s0_f, s0_b, consts, *, batch, seq, row_block0, heads, dk):
    bmf, bmb, mkf, mkb, nlev = consts
    c = REC_CHUNK
    has_s0 = s0_f is not None

    def col(group):
        return pl.BlockSpec((seq, dk), lambda b, h: (row_block0 + b, group * heads + h))

    def const(arr):
        return pl.BlockSpec(arr.shape, lambda b, h: (0,) * arr.ndim)

    state_spec = pl.BlockSpec((None, None, dk, dk), lambda b, h: (b, h, 0, 0))
    per_head = pl.BlockSpec((None, 1, dk), lambda b, h: (h, 0, 0))
    in_specs = [col(0), col(1), col(2), col(3), col(4), per_head, per_head,
                pl.BlockSpec((1, dk), lambda b, h: (0, 0)),
                const(bmf), const(bmb), const(mkf), const(mkb)]
    args = [proj, proj, proj, proj, proj, lb_f.reshape(heads, 1, dk), lb_b.reshape(heads, 1, dk),
            norm_w.reshape(1, dk), bmf, bmb, mkf, mkb]
    if has_s0:
        in_specs += [state_spec, state_spec]
        args += [s0_f, s0_b]
    return pl.pallas_call(
        functools.partial(_hgrn_kernel, c=c, nchunks=seq // c, nlev=nlev, scale=dk ** -0.5, has_s0=has_s0),
        grid=(batch, heads),
        in_specs=in_specs,
        out_specs=[pl.BlockSpec((seq, dk), lambda b, h: (b, h)), state_spec, state_spec],
        out_shape=[jax.ShapeDtypeStruct((batch * seq, heads * dk), BF16),
                   jax.ShapeDtypeStruct((batch, heads, dk, dk), F32),
                   jax.ShapeDtypeStruct((batch, heads, dk, dk), F32)],
        scratch_shapes=[pltpu.VMEM((seq, dk), F32), pltpu.VMEM((seq, dk), F32),
                        pltpu.VMEM((dk, dk), F32), pltpu.VMEM((dk, dk), F32)],
        compiler_params=_cparams("arbitrary", "arbitrary"),
        name="hgrn_scan",
    )(*args)


def _gla_kernel(*refs, c, nchunks, nlev, scale, has_s0):
    (q_ref, k_ref, v_ref, g_ref, lr_ref, ufh_ref, ufl_ref, ubh_ref, ubl_ref, bf_ref, bb_ref, nw_ref,
     bmf_ref, bmb_ref, mkf_ref, mkb_ref) = refs[:16]
    refs = refs[16:]
    if has_s0:
        s0f_ref, s0b_ref = refs[:2]
        refs = refs[2:]
    out_ref, sf_ref, sb_ref, of_scr, ob_scr, stf_scr, stb_scr = refs

    if has_s0:
        stf_scr[...] = s0f_ref[...].T
        stb_scr[...] = s0b_ref[...].T
    else:
        stf_scr[...] = jnp.zeros_like(stf_scr)
        stb_scr[...] = jnp.zeros_like(stb_scr)

    def one_direction(chunk, uh_ref, ul_ref, bias_ref, st_scr, bm_ref, mk_ref, o_scr, total_row):
        rows = pl.ds(pl.multiple_of(chunk * c, c), c)
        lr_hi, lr_lo = _split_hi_lo(lr_ref[rows, :])
        z = _dot(lr_hi, uh_ref[...]) + _dot(lr_hi, ul_ref[...]) + _dot(lr_lo, uh_ref[...]) + bias_ref[...]
        nl = (jnp.maximum(-z, 0.0) + jnp.log(1.0 + jnp.exp(-jnp.abs(z)))) * (1.0 / GLA_LOGIT_NORM)
        o_scr[rows, :] = _chunk_step(q_ref[rows, :] * scale, k_ref[rows, :], v_ref[rows, :].astype(BF16),
                                     nl, st_scr, bm_ref, mk_ref, c, nlev, total_row)

    def body(ci, carry):
        one_direction(ci, ufh_ref, ufl_ref, bf_ref, stf_scr, bmf_ref, mkf_ref, of_scr, c - 1)
        one_direction(nchunks - 1 - ci, ubh_ref, ubl_ref, bb_ref, stb_scr, bmb_ref, mkb_ref, ob_scr, 0)
        return carry

    lax.fori_loop(0, nchunks, body, 0)
    _head_norm_out(of_scr, ob_scr, g_ref, nw_ref, out_ref)
    sf_ref[...] = stf_scr[...].T
    sb_ref[...] = stb_scr[...].T


def _gla(proj, lr, up_f, up_b, bias_f, bias_b, norm_w, s0_f, s0_b, consts, *,
         batch, seq, row_block0, heads, dk, dv, col0):
    bmf, bmb, mkf, mkb, nlev = consts
    c = REC_CHUNK
    has_s0 = s0_f is not None
    kw = heads * dk
    q0, k0, v0, g0 = col0 // dk, (col0 + kw) // dk, (col0 + 2 * kw) // dv, (col0 + 2 * kw + heads * dv) // dv

    def const(arr):
        return pl.BlockSpec(arr.shape, lambda b, h: (0,) * arr.ndim)

    def head_cols(arr):
        return pl.BlockSpec((arr.shape[0], dk), lambda b, h: (0, h))

    state_spec = pl.BlockSpec((None, None, dk, dv), lambda b, h: (b, h, 0, 0))
    ufh, ufl = _split_hi_lo(up_f)
    ubh, ubl = _split_hi_lo(up_b)
    in_specs = [pl.BlockSpec((seq, dk), lambda b, h: (row_block0 + b, q0 + h)),
                pl.BlockSpec((seq, dk), lambda b, h: (row_block0 + b, k0 + h)),
                pl.BlockSpec((seq, dv), lambda b, h: (row_block0 + b, v0 + h)),
                pl.BlockSpec((seq, dv), lambda b, h: (row_block0 + b, g0 + h)),
                pl.BlockSpec((seq, LANES), lambda b, h: (row_block0 + b, 0)),
                head_cols(ufh), head_cols(ufl), head_cols(ubh), head_cols(ubl),
                head_cols(bias_f), head_cols(bias_b),
                pl.BlockSpec((1, dv), lambda b, h: (0, 0)),
                const(bmf), const(bmb), const(mkf), const(mkb)]
    args = [proj, proj, proj, proj, lr, ufh, ufl, ubh, ubl, bias_f, bias_b, norm_w.reshape(1, dv),
            bmf, bmb, mkf, mkb]
    if has_s0:
        in_specs += [state_spec, state_spec]
        args += [s0_f, s0_b]
    return pl.pallas_call(
        functools.partial(_gla_kernel, c=c, nchunks=seq // c, nlev=nlev, scale=dk ** -0.5, has_s0=has_s0),
        grid=(batch, heads),
        in_specs=in_specs,
        out_specs=[pl.BlockSpec((seq, dv), lambda b, h: (b, h)), state_spec, state_spec],
        out_shape=[jax.ShapeDtypeStruct((batch * seq, heads * dv), BF16),
                   jax.ShapeDtypeStruct((batch, heads, dk, dv), F32),
                   jax.ShapeDtypeStruct((batch, heads, dk, dv), F32)],
        scratch_shapes=[pltpu.VMEM((seq, dv), F32), pltpu.VMEM((seq, dv), F32),
                        pltpu.VMEM((dv, dk), F32), pltpu.VMEM((dv, dk), F32)],
        compiler_params=_cparams("arbitrary", "arbitrary"),
        name="gla_scan",
    )(*args)


def _router_kernel(h_ref, whi_ref, wlo_ref, o_ref):
    a_hi, a_lo = _split_hi_lo(h_ref[...])
    z = _dot(a_hi, whi_ref[...]) + _dot(a_hi, wlo_ref[...]) + _dot(a_lo, whi_ref[...])
    o_ref[...] = jax.nn.sigmoid(z)


def _router_scores(h, router_w, tm):
    t, d = h.shape
    e = router_w.shape[1]
    w_hi, w_lo = _split_hi_lo(router_w)
    return pl.pallas_call(
        _router_kernel,
        grid=(t // tm,),
        in_specs=[pl.BlockSpec((tm, d), lambda i: (i, 0)),
                  pl.BlockSpec((d, e), lambda i: (0, 0)),
                  pl.BlockSpec((d, e), lambda i: (0, 0))],
        out_specs=pl.BlockSpec((tm, e), lambda i: (i, 0)),
        out_shape=jax.ShapeDtypeStruct((t, e), F32),
        compiler_params=_cparams("arbitrary"),
        name="router",
    )(h, w_hi, w_lo)


def _route(scores, router_bias):
    t, e = scores.shape
    biased = scores + router_bias[None, :]
    grp_score = lax.top_k(biased.reshape(t, N_GROUP, e // N_GROUP), 2)[0].sum(-1)
    _, gidx = lax.top_k(grp_score, TOPK_GROUP)
    gmask = jnp.any(gidx[:, :, None] == jnp.arange(N_GROUP)[None, None, :], axis=1)
    emask = jnp.repeat(gmask, e // N_GROUP, axis=1)
    _, eidx = lax.top_k(jnp.where(emask, biased, -jnp.inf), TOP_K)
    wts = jnp.take_along_axis(scores, eidx, axis=1)
    wts = wts / jnp.sum(wts, axis=-1, keepdims=True) * ROUTED_SCALE
    return eidx, wts


def _expert_layout(eidx, n_experts, unit_rows):
    t, k = eidx.shape
    n_assign = t * k
    flat_e = eidx.reshape(n_assign).astype(jnp.int32)
    order = jnp.argsort(flat_e).astype(jnp.int32)
    sorted_e = flat_e[order]
    counts = jnp.bincount(flat_e, length=n_experts).astype(jnp.int32)
    padded = (counts + ROW_ALIGN - 1) // ROW_ALIGN * ROW_ALIGN
    seg_end = jnp.cumsum(padded)
    seg_start = seg_end - padded
    first = jnp.cumsum(counts) - counts
    slot = seg_start[sorted_e] + jnp.arange(n_assign, dtype=jnp.int32) - first[sorted_e]
    n_slots = n_assign + n_experts * ROW_ALIGN
    slot_tok = jnp.zeros((n_slots,), jnp.int32).at[slot].set(order // k)
    assign_slot = jnp.zeros((n_assign,), jnp.int32).at[order].set(slot)
    units_per_e = (counts + unit_rows - 1) // unit_rows
    unit_end = jnp.cumsum(units_per_e)
    unit_first = unit_end - units_per_e
    n_units = unit_end[-1]
    max_units = n_experts + n_assign // unit_rows
    uid = jnp.arange(max_units, dtype=jnp.int32)
    ue = jnp.minimum(jnp.searchsorted(unit_end, uid, side="right"), n_experts - 1).astype(jnp.int32)
    j = uid - unit_first[ue]
    valid = uid < n_units
    last_e = ue[jnp.maximum(n_units - 1, 0)]
    unit_e = jnp.where(valid, ue, last_e)
    unit_start = jnp.where(valid, seg_start[ue] + j * unit_rows, 0)
    unit_nrows = jnp.where(valid, jnp.clip(padded[ue] - j * unit_rows, 0, unit_rows), 0)
    n_units_tail = jnp.stack([n_units, seg_end[-1]]).astype(jnp.int32)
    return (slot_tok, assign_slot, unit_e.astype(jnp.int32), unit_start.astype(jnp.int32),
            unit_nrows.astype(jnp.int32), n_units_tail, n_slots, max_units)


def _moe_kernel(ue_ref, us_ref, ur_ref, nu_ref, tok_ref,
                x_hbm, wg_ref, wu_ref, wd_ref, y_hbm,
                xbuf, yacc, zrows, wgb, wub, wdb, gsem, osem, *, bm, nf):
    u = pl.program_id(0)
    f = pl.program_id(1)
    d = xbuf.shape[1]

    def row_group_copy(src, dst, sem):
        return pltpu.make_async_copy(src, dst, sem)

    @pl.when(jnp.logical_and(u == 0, f == 0))
    def _():
        xbuf[...] = jnp.zeros_like(xbuf)
        zrows[...] = jnp.zeros_like(zrows)
        tail0 = pl.multiple_of(nu_ref[1], ROW_ALIGN)
        tail_groups = (y_hbm.shape[0] - tail0) // ROW_ALIGN

        def tail_copy(g):
            r0 = pl.multiple_of(tail0 + g * ROW_ALIGN, ROW_ALIGN)
            return row_group_copy(zrows, y_hbm.at[pl.ds(r0, ROW_ALIGN)], osem)

        def tail_issue(g, carry):
            tail_copy(g).start()
            return carry

        def tail_wait(g, carry):
            tail_copy(g).wait()
            return carry

        lax.fori_loop(0, tail_groups, tail_issue, 0)
        lax.fori_loop(0, tail_groups, tail_wait, 0)

    @pl.when(u < nu_ref[0])
    def _():
        start = pl.multiple_of(us_ref[u], ROW_ALIGN)
        rows = ur_ref[u]
        groups = rows // ROW_ALIGN

        @pl.when(f == 0)
        def _():
            def issue(r, carry):
                t = tok_ref[start + r]
                row_group_copy(x_hbm.at[pl.ds(t, 1)], xbuf.at[pl.ds(r, 1)], gsem).start()
                return carry

            lax.fori_loop(0, rows, issue, 0)

            def wait(g, carry):
                row_group_copy(x_hbm.at[pl.ds(0, ROW_ALIGN)], xbuf.at[pl.ds(0, ROW_ALIGN)], gsem).wait()
                return carry

            lax.fori_loop(0, groups, wait, 0)

        wgb[...] = wg_ref[...].astype(BF16)
        wub[...] = wu_ref[...].astype(BF16)
        wdb[...] = wd_ref[...].astype(BF16)

        def block(rb, carry):
            r0 = pl.multiple_of(rb * bm, bm)
            xb = xbuf[pl.ds(r0, bm), :].astype(BF16)
            g = _dot(xb, wgb[...])
            up = _dot(xb, wub[...])
            y = _dot((_silu(g) * up).astype(BF16), wdb[...])

            @pl.when(f == 0)
            def _():
                yacc[pl.ds(r0, bm), :] = y

            @pl.when(f > 0)
            def _():
                yacc[pl.ds(r0, bm), :] += y

            return carry

        lax.fori_loop(0, (rows + bm - 1) // bm, block, 0)

        @pl.when(f == nf - 1)
        def _():
            def out_copy(g):
                r0 = pl.multiple_of(g * ROW_ALIGN, ROW_ALIGN)
                return row_group_copy(yacc.at[pl.ds(r0, ROW_ALIGN)],
                                      y_hbm.at[pl.ds(start + r0, ROW_ALIGN)], osem)

            def issue(g, carry):
                out_copy(g).start()
                return carry

            def wait(g, carry):
                out_copy(g).wait()
                return carry

            lax.fori_loop(0, groups, issue, 0)
            lax.fori_loop(0, groups, wait, 0)


def _grouped_swiglu(x, slot_tok, unit_e, unit_start, unit_nrows, n_units, n_slots, max_units,
                    w_gate, w_up, w_down, unit_rows, name):
    t, d = x.shape
    n_e, _, de = w_gate.shape
    tf = min(MOE_F_TILE, de)
    nf = de // tf
    bm = min(MOE_ROW_BLOCK, unit_rows)

    def f_eff(u, f, nu):
        return jnp.where(u < nu[0], f, nf - 1)

    grid_spec = pltpu.PrefetchScalarGridSpec(
        num_scalar_prefetch=5,
        grid=(max_units, nf),
        in_specs=[pl.BlockSpec(memory_space=pl.ANY),
                  pl.BlockSpec((None, d, tf), lambda u, f, ue, us, ur, nu, tok: (ue[u], 0, f_eff(u, f, nu))),
                  pl.BlockSpec((None, d, tf), lambda u, f, ue, us, ur, nu, tok: (ue[u], 0, f_eff(u, f, nu))),
                  pl.BlockSpec((None, tf, d), lambda u, f, ue, us, ur, nu, tok: (ue[u], f_eff(u, f, nu), 0))],
        out_specs=pl.BlockSpec(memory_space=pl.ANY),
        scratch_shapes=[pltpu.VMEM((unit_rows, d), F32), pltpu.VMEM((unit_rows, d), F32),
                        pltpu.VMEM((ROW_ALIGN, d), F32), pltpu.VMEM((d, tf), BF16), pltpu.VMEM((d, tf), BF16), pltpu.VMEM((tf, d), BF16),
                        pltpu.SemaphoreType.DMA(()), pltpu.SemaphoreType.DMA(())])
    return pl.pallas_call(
        functools.partial(_moe_kernel, bm=bm, nf=nf),
        grid_spec=grid_spec,
        out_shape=jax.ShapeDtypeStruct((n_slots, d), F32),
        compiler_params=_cparams("arbitrary", "arbitrary"),
        name=name,
    )(unit_e, unit_start, unit_nrows, n_units, slot_tok, x, w_gate, w_up, w_down)


def _combine_kernel(pos_ref, ys_hbm, wts_ref, ysh_ref, x1_ref, mod_ref, fw_ref, out_ref, buf, sem, *, tm, k):
    i = pl.program_id(0)

    def issue(r, carry):
        base = (i * tm + r) * k
        for kk in range(k):
            pltpu.make_async_copy(ys_hbm.at[pl.ds(pos_ref[base + kk], 1)], buf.at[kk, pl.ds(r, 1)], sem).start()
        return carry

    lax.fori_loop(0, tm, issue, 0)
    for kk in range(k):
        pltpu.make_async_copy(ys_hbm.at[pl.ds(0, tm)], buf.at[kk], sem).wait()
    acc = buf[0] * wts_ref[:, 0:1]
    for kk in range(1, k):
        acc = acc + buf[kk] * wts_ref[:, kk:kk + 1]
    ff = acc + ysh_ref[...]
    x2 = x1_ref[...] + mod_ref[0, 5:6, :] * ff
    out_ref[...] = x2 * lax.rsqrt(jnp.mean(x2 * x2, axis=-1, keepdims=True) + EPS) * fw_ref[...]


def _combine(y_sorted, assign_slot, wts, y_shared, x1, mod3, final_w, tm, ctx_rows, lat_seq):
    t, d = x1.shape
    k = wts.shape[1]
    cb = ctx_rows // tm
    bps = lat_seq // tm
    ctx_row = (t - ctx_rows) // lat_seq
    grid_spec = pltpu.PrefetchScalarGridSpec(
        num_scalar_prefetch=1,
        grid=(t // tm,),
        in_specs=[pl.BlockSpec(memory_space=pl.ANY),
                  pl.BlockSpec((tm, k), lambda i, pos: (i, 0)),
                  pl.BlockSpec((tm, d), lambda i, pos: (i, 0)),
                  pl.BlockSpec((tm, d), lambda i, pos: (i, 0)),
                  pl.BlockSpec((1, 6, d), _mod_row_map(cb, bps, ctx_row)),
                  pl.BlockSpec((1, d), lambda i, pos: (0, 0))],
        out_specs=pl.BlockSpec((tm, d), lambda i, pos: (i, 0)),
        scratch_shapes=[pltpu.VMEM((k, tm, d), F32), pltpu.SemaphoreType.DMA(())])
    return pl.pallas_call(
        functools.partial(_combine_kernel, tm=tm, k=k),
        grid_spec=grid_spec,
        out_shape=jax.ShapeDtypeStruct((t, d), F32),
        compiler_params=_cparams("arbitrary"),
        name="moe_combine",
    )(assign_slot, y_sorted, wts, y_shared, x1, mod3, final_w)


def _grid_pos_embed(n_tokens, d_model):
    rows = n_tokens // GRID_W
    r, col = jnp.meshgrid(jnp.arange(rows, dtype=F32), jnp.arange(GRID_W, dtype=F32), indexing="ij")
    n_freq = d_model // 4
    omega = 1.0 / (10000.0 ** (jnp.arange(n_freq, dtype=F32) / n_freq))

    def enc(p):
        ang = p.reshape(-1)[:, None] * omega[None, :]
        return jnp.concatenate([jnp.sin(ang), jnp.cos(ang)], axis=-1)

    return jnp.concatenate([enc(r), enc(col)], axis=-1)


def _pad_rows(a, rows, row0=0):
    return jnp.zeros((rows,) + a.shape[1:], a.dtype).at[row0:row0 + a.shape[0]].set(a)


def kernel(x_prompt, x_sample, state_hgrn_fwd, state_hgrn_bwd, state_gla_fwd, state_gla_bwd, c, c_ctx, norm1_w, norm2_w, ada_w, ada_b, w_in, hg_lb_logits, gla_gk_up_fwd, gla_gk_bias_fwd, gla_gk_up_bwd, gla_gk_bias_bwd, hg_norm_w, gla_norm_w, w_out, router_w, router_bias, exp_w_gate, exp_w_up, exp_w_down, sh_w_gate, sh_w_up, sh_w_down, final_norm_w):
    n_ctx, ctx_seq, d = x_prompt.shape
    n_lat, lat_seq, _ = x_sample.shape
    depth = norm1_w.shape[0]
    assert depth == 1, "one trunk layer"
    _, _, hg_heads, hg_dk, hg_dv = state_hgrn_fwd.shape
    _, _, gla_heads, gla_dk, gla_dv = state_gla_fwd.shape
    assert hg_dk == hg_dv
    rank = gla_gk_up_fwd.shape[1]
    n_experts = exp_w_gate.shape[1]
    hg_w = hg_heads * hg_dk
    gla_kw = gla_heads * gla_dk
    gla_w = gla_heads * gla_dv
    tp, ts = n_ctx * ctx_seq, n_lat * lat_seq
    t = tp + ts
    assert tp % lat_seq == 0 and 2 * rank <= LANES and n_lat < 8
    tm = min(256, ctx_seq)
    layer = 0

    lb_all = jnp.cumsum(jax.nn.softmax(hg_lb_logits.astype(F32), axis=1), axis=1)
    pos = _grid_pos_embed(lat_seq, d)

    c_rows = jnp.zeros((8, d), F32).at[:n_lat].set(c).at[n_lat].set(c_ctx)
    mod = _ada(c_rows, ada_w[layer], ada_b[layer].reshape(1, -1))
    mod3 = mod.reshape(8, 6, d)

    x0, h1 = _norm1(x_prompt.reshape(tp, d), x_sample.reshape(ts, d), pos, mod3,
                    norm1_w[layer].reshape(1, d), tm, lat_seq)

    lr0 = 5 * hg_w + 2 * gla_kw + gla_w
    w_l = w_in[layer]
    w_main = jnp.concatenate([w_l[:, :lr0], w_l[:, lr0 + 2 * rank:]], axis=1).astype(BF16)
    w_lr = jnp.zeros((d, LANES), BF16).at[:, :2 * rank].set(w_l[:, lr0:lr0 + 2 * rank].astype(BF16))
    proj = _matmul(h1, w_main, 1024, 1024, "in_proj")
    lr = _matmul(h1, w_lr, 1024, LANES, "in_proj_lr")

    consts = _level_constants(REC_CHUNK)
    up_f = _pad_rows(gla_gk_up_fwd[layer], LANES, 0)
    up_b = _pad_rows(gla_gk_up_bwd[layer], LANES, rank)
    bias_f = gla_gk_bias_fwd[layer].reshape(1, gla_kw)
    bias_b = gla_gk_bias_bwd[layer].reshape(1, gla_kw)
    hg_args = dict(heads=hg_heads, dk=hg_dk)
    gla_args = dict(heads=gla_heads, dk=gla_dk, dv=gla_dv, col0=5 * hg_w)

    hg_ctx, new_hf, new_hb = _hgrn(proj, lb_all[0, layer], lb_all[1, layer], hg_norm_w[layer], None, None, consts,
                                   batch=n_ctx, seq=ctx_seq, row_block0=0, **hg_args)
    hg_lat, _, _ = _hgrn(proj, lb_all[0, layer], lb_all[1, layer], hg_norm_w[layer],
                         state_hgrn_fwd[:, layer], state_hgrn_bwd[:, layer], consts,
                         batch=n_lat, seq=lat_seq, row_block0=tp // lat_seq, **hg_args)
    gla_ctx, new_gf, new_gb = _gla(proj, lr, up_f, up_b, bias_f, bias_b, gla_norm_w[layer], None, None, consts,
                                   batch=n_ctx, seq=ctx_seq, row_block0=0, **gla_args)
    gla_lat, _, _ = _gla(proj, lr, up_f, up_b, bias_f, bias_b, gla_norm_w[layer],
                         state_gla_fwd[:, layer], state_gla_bwd[:, layer], consts,
                         batch=n_lat, seq=lat_seq, row_block0=tp // lat_seq, **gla_args)

    w_o = w_out[layer].astype(BF16)
    mix_ctx = _matmul2(hg_ctx, w_o[:hg_w], gla_ctx, w_o[hg_w:], 1024, 1024, "out_proj_ctx")
    mix_lat = _matmul2(hg_lat, w_o[:hg_w], gla_lat, w_o[hg_w:], 1024, 1024, "out_proj_lat")

    x1, h2 = _norm2(x0, mix_ctx, mix_lat, mod3, norm2_w[layer].reshape(1, d), tm, lat_seq)

    scores = _router_scores(h2, router_w[layer], 512)
    eidx, wts = _route(scores, router_bias[layer].astype(F32))
    slot_tok, assign_slot, unit_e, unit_start, unit_nrows, n_units, n_slots, max_units = _expert_layout(
        eidx, n_experts, MOE_UNIT_ROWS)
    y_sorted = _grouped_swiglu(h2, slot_tok, unit_e, unit_start, unit_nrows, n_units, n_slots, max_units,
                               exp_w_gate[layer], exp_w_up[layer], exp_w_down[layer], MOE_UNIT_ROWS, "moe_experts")

    assert t % MOE_UNIT_ROWS == 0
    sh_units = t // MOE_UNIT_ROWS
    sh_uid = jnp.arange(sh_units, dtype=jnp.int32)
    y_shared = _grouped_swiglu(h2, jnp.arange(t, dtype=jnp.int32), jnp.zeros((sh_units,), jnp.int32),
                               sh_uid * MOE_UNIT_ROWS, jnp.full((sh_units,), MOE_UNIT_ROWS, jnp.int32),
                               jnp.array([sh_units, t], jnp.int32), t, sh_units,
                               sh_w_gate[layer][None], sh_w_up[layer][None], sh_w_down[layer][None],
                               MOE_UNIT_ROWS, "shared_expert")

    y = _combine(y_sorted, assign_slot, wts, y_shared, x1, mod3, final_norm_w.reshape(1, d), 64, tp, lat_seq)
    y_prompt = y[:tp].reshape(n_ctx, ctx_seq, d)
    y_sample = y[tp:].reshape(n_lat, lat_seq, d)
    return (y_prompt, y_sample, new_hf[:, None], new_hb[:, None], new_gf[:, None], new_gb[:, None])
```

```python
import functools
import math

import numpy as np
import jax
import jax.numpy as jnp
from jax import lax
from jax.experimental import pallas as pl
from jax.experimental.pallas import tpu as pltpu

F32 = jnp.float32
BF16 = jnp.bfloat16

GRID_W = 64
TOP_K = 8
N_GROUP = 8
TOPK_GROUP = 4
ROUTED_SCALE = 2.5
GLA_LOGIT_NORM = 16.0
EPS = 1e-6

V7X_VMEM_LIMIT_BYTES = 56 * 1024 * 1024
LANES = 128
ROW_ALIGN = 16
REC_CHUNK = 128
MOE_UNIT_ROWS = 512
MOE_ROW_BLOCK = 128
MOE_F_TILE = 256
MOE_OUT_TILE = 1024


def _cparams(*sem):
    return pltpu.CompilerParams(dimension_semantics=sem, vmem_limit_bytes=V7X_VMEM_LIMIT_BYTES)


def _tile(n, preferred):
    t = min(preferred, n)
    while n % t:
        t //= 2
    return t


def _dot(a, b):
    return jnp.dot(a, b, preferred_element_type=F32)


def _dot_nt(a, b):
    return lax.dot_general(a, b, (((1,), (1,)), ((), ())), preferred_element_type=F32)


def _dot_tn(a, b):
    return lax.dot_general(a, b, (((0,), (0,)), ((), ())), preferred_element_type=F32)


def _silu(x):
    return x * jax.nn.sigmoid(x)


def _split_hi_lo(x):
    hi = x.astype(BF16)
    return hi, (x - hi.astype(F32)).astype(BF16)


def _ada_kernel(c_ref, w_ref, b_ref, o_ref):
    a = _silu(c_ref[...]).astype(BF16)
    o_ref[...] = _dot(a, w_ref[...].astype(BF16)) + b_ref[...]


def _ada(c_rows, ada_w, ada_b):
    d, n = ada_w.shape
    rows = c_rows.shape[0]
    tn = min(512, n)
    return pl.pallas_call(
        _ada_kernel,
        grid=(n // tn,),
        in_specs=[pl.BlockSpec((rows, d), lambda j: (0, 0)),
                  pl.BlockSpec((d, tn), lambda j: (0, j)),
                  pl.BlockSpec((1, tn), lambda j: (0, j))],
        out_specs=pl.BlockSpec((rows, tn), lambda j: (0, j)),
        out_shape=jax.ShapeDtypeStruct((rows, n), F32),
        compiler_params=_cparams("arbitrary"),
        name="ada_mod",
    )(c_rows, ada_w, ada_b)


def _modulated_norm(x, w, shift, scale):
    y = x * lax.rsqrt(jnp.mean(x * x, axis=-1, keepdims=True) + EPS) * w
    return y * (1.0 + scale) + shift


def _norm1_kernel(xp_ref, xs_ref, pos_ref, mod_ref, w_ref, x0_ref, h_ref, *, ctx_blocks):
    i = pl.program_id(0)

    def finish(x):
        x0_ref[...] = x
        h_ref[...] = _modulated_norm(x, w_ref[...], mod_ref[0, 0:1, :], mod_ref[0, 1:2, :]).astype(BF16)

    @pl.when(i < ctx_blocks)
    def _():
        finish(xp_ref[...])

    @pl.when(i >= ctx_blocks)
    def _():
        finish(xs_ref[...] + pos_ref[...])


def _mod_row_map(ctx_blocks, blocks_per_seq, ctx_row):
    return lambda i, *_: (jnp.where(i < ctx_blocks, ctx_row, (i - ctx_blocks) // blocks_per_seq), 0, 0)


def _norm1(xp, xs, pos, mod3, w, tm, lat_seq):
    tp, d = xp.shape
    ts = xs.shape[0]
    t = tp + ts
    cb = tp // tm
    bps = lat_seq // tm
    ctx_row = ts // lat_seq
    return pl.pallas_call(
        functools.partial(_norm1_kernel, ctx_blocks=cb),
        grid=(t // tm,),
        in_specs=[pl.BlockSpec((tm, d), lambda i: (jnp.minimum(i, cb - 1), 0)),
                  pl.BlockSpec((tm, d), lambda i: (jnp.maximum(i - cb, 0), 0)),
                  pl.BlockSpec((tm, d), lambda i: (jnp.maximum(i - cb, 0) % bps, 0)),
                  pl.BlockSpec((1, 6, d), _mod_row_map(cb, bps, ctx_row)),
                  pl.BlockSpec((1, d), lambda i: (0, 0))],
        out_specs=[pl.BlockSpec((tm, d), lambda i: (i, 0)),
                   pl.BlockSpec((tm, d), lambda i: (i, 0))],
        out_shape=[jax.ShapeDtypeStruct((t, d), F32), jax.ShapeDtypeStruct((t, d), BF16)],
        compiler_params=_cparams("arbitrary"),
        name="norm1",
    )(xp, xs, pos, mod3, w)


def _pack_bf16_pairs(h):
    half = h.shape[1] // 2
    bits = pltpu.bitcast(h.astype(BF16).astype(F32), jnp.uint32)
    return (bits[:, :half] >> 16) | (bits[:, half:] & jnp.uint32(0xFFFF0000))


def _unpack_bf16_pairs(w):
    lo = pltpu.bitcast(w << 16, F32).astype(BF16)
    hi = pltpu.bitcast(w & jnp.uint32(0xFFFF0000), F32).astype(BF16)
    return jnp.concatenate([lo, hi], axis=1)


def _norm2_kernel(x0_ref, mc_ref, ml_ref, mod_ref, w_ref, x1_ref, h_ref, hp_ref, *, ctx_blocks):
    i = pl.program_id(0)

    def finish(mix):
        x1 = x0_ref[...] + mod_ref[0, 2:3, :] * mix
        x1_ref[...] = x1
        h = _modulated_norm(x1, w_ref[...], mod_ref[0, 3:4, :], mod_ref[0, 4:5, :])
        h_ref[...] = h
        hp_ref[...] = _pack_bf16_pairs(h)

    @pl.when(i < ctx_blocks)
    def _():
        finish(mc_ref[...])

    @pl.when(i >= ctx_blocks)
    def _():
        finish(ml_ref[...])


def _norm2(x0, mix_ctx, mix_lat, mod3, w, tm, lat_seq):
    t, d = x0.shape
    tp = mix_ctx.shape[0]
    cb = tp // tm
    bps = lat_seq // tm
    ctx_row = (t - tp) // lat_seq
    return pl.pallas_call(
        functools.partial(_norm2_kernel, ctx_blocks=cb),
        grid=(t // tm,),
        in_specs=[pl.BlockSpec((tm, d), lambda i: (i, 0)),
                  pl.BlockSpec((tm, d), lambda i: (jnp.minimum(i, cb - 1), 0)),
                  pl.BlockSpec((tm, d), lambda i: (jnp.maximum(i - cb, 0), 0)),
                  pl.BlockSpec((1, 6, d), _mod_row_map(cb, bps, ctx_row)),
                  pl.BlockSpec((1, d), lambda i: (0, 0))],
        out_specs=[pl.BlockSpec((tm, d), lambda i: (i, 0)),
                   pl.BlockSpec((tm, d), lambda i: (i, 0)),
                   pl.BlockSpec((tm, d // 2), lambda i: (i, 0))],
        out_shape=[jax.ShapeDtypeStruct((t, d), F32), jax.ShapeDtypeStruct((t, d), F32),
                   jax.ShapeDtypeStruct((t, d // 2), jnp.uint32)],
        compiler_params=_cparams("arbitrary"),
        name="norm2",
    )(x0, mix_ctx, mix_lat, mod3, w)


def _mm_kernel(a_ref, b_ref, o_ref):
    o_ref[...] = _dot(a_ref[...], b_ref[...])


def _matmul(a, b, tm, tn, name):
    m, k = a.shape
    n = b.shape[1]
    tm, tn = _tile(m, tm), _tile(n, tn)
    return pl.pallas_call(
        _mm_kernel,
        grid=(n // tn, m // tm),
        in_specs=[pl.BlockSpec((tm, k), lambda j, i: (i, 0)),
                  pl.BlockSpec((k, tn), lambda j, i: (0, j))],
        out_specs=pl.BlockSpec((tm, tn), lambda j, i: (i, j)),
        out_shape=jax.ShapeDtypeStruct((m, n), F32),
        compiler_params=_cparams("arbitrary", "arbitrary"),
        name=name,
    )(a, b)


def _mm2_kernel(a1_ref, b1_ref, a2_ref, b2_ref, o_ref):
    o_ref[...] = _dot(a1_ref[...], b1_ref[...]) + _dot(a2_ref[...], b2_ref[...])


def _matmul2(a1, b1, a2, b2, tm, tn, name):
    m, k1 = a1.shape
    k2 = a2.shape[1]
    n = b1.shape[1]
    tm, tn = _tile(m, tm), _tile(n, tn)
    return pl.pallas_call(
        _mm2_kernel,
        grid=(n // tn, m // tm),
        in_specs=[pl.BlockSpec((tm, k1), lambda j, i: (i, 0)),
                  pl.BlockSpec((k1, tn), lambda j, i: (0, j)),
                  pl.BlockSpec((tm, k2), lambda j, i: (i, 0)),
                  pl.BlockSpec((k2, tn), lambda j, i: (0, j))],
        out_specs=pl.BlockSpec((tm, tn), lambda j, i: (i, j)),
        out_shape=jax.ShapeDtypeStruct((m, n), F32),
        compiler_params=_cparams("arbitrary", "arbitrary"),
        name=name,
    )(a1, b1, a2, b2)


def _level_constants(c):
    nlev = int(math.log2(c))
    i = np.arange(c)[:, None]
    t = np.arange(c)[None, :]
    groups = [t <= i, t > i]
    masks = [i == t]
    for lev in range(nlev):
        s = 1 << lev
        base = (i // (2 * s)) * (2 * s)
        pivot = base + s - 1
        right = i >= base + s
        groups.append(np.where(right, (t > pivot) & (t <= i), (t > i) & (t <= pivot)))
        same = (i // (2 * s)) == (t // (2 * s))
        masks.append(same & right & ((t % (2 * s)) < s))
    bmat_f = np.concatenate([g.astype(np.float32) for g in groups], axis=0)
    bmat_b = np.concatenate([g[::-1, ::-1].astype(np.float32) for g in groups], axis=0)
    mask_f = np.stack([m.astype(np.float32) for m in masks])
    mask_b = np.stack([m[::-1, ::-1].astype(np.float32) for m in masks])
    return (jnp.asarray(bmat_f, BF16), jnp.asarray(bmat_b, BF16),
            jnp.asarray(mask_f, F32), jnp.asarray(mask_b, F32), nlev)


def _chunk_step(q_s, k, v_bf, nl, st_ref, bmat_ref, mask_ref, c, nlev, total_row):
    n1 = nl.astype(BF16)
    r1 = nl - n1.astype(F32)
    n2 = r1.astype(BF16)
    n3 = (r1 - n2.astype(F32)).astype(BF16)
    bm = bmat_ref[...]
    dec = _dot(bm, n1) + _dot(bm, n2) + _dot(bm, n3)
    cum = dec[0:c]
    tail = dec[c:2 * c]
    st = st_ref[...]
    o = _dot_nt((q_s * jnp.exp(-cum)).astype(BF16), st.astype(BF16))
    att = mask_ref[0] * _dot_nt(q_s.astype(BF16), k.astype(BF16))
    for lev in range(nlev):
        e = jnp.exp(-dec[(2 + lev) * c:(3 + lev) * c])
        att = att + mask_ref[1 + lev] * _dot_nt((q_s * e).astype(BF16), (k * e).astype(BF16))
    o = o + _dot(att.astype(BF16), v_bf)
    k_hat = (k * jnp.exp(-tail)).astype(BF16)
    total = cum[total_row:total_row + 1]
    st_ref[...] = st * jnp.exp(-total) + _dot_tn(v_bf, k_hat)
    return o


def _head_norm_out(of_scr, ob_scr, g_ref, nw_ref, out_ref):
    o = of_scr[...] + ob_scr[...]
    y = o * lax.rsqrt(jnp.mean(o * o, axis=-1, keepdims=True) + EPS) * nw_ref[...]
    out_ref[...] = (y * _silu(g_ref[...])).astype(BF16)


def _hgrn_kernel(*refs, c, nchunks, nlev, scale, has_s0):
    (q_ref, ff_ref, fb_ref, i_ref, g_ref, lbf_ref, lbb_ref, nw_ref,
     bmf_ref, bmb_ref, mkf_ref, mkb_ref) = refs[:12]
    refs = refs[12:]
    if has_s0:
        s0f_ref, s0b_ref = refs[:2]
        refs = refs[2:]
    out_ref, sf_ref, sb_ref, of_scr, ob_scr, stf_scr, stb_scr = refs

    if has_s0:
        stf_scr[...] = s0f_ref[...].T
        stb_scr[...] = s0b_ref[...].T
    else:
        stf_scr[...] = jnp.zeros_like(stf_scr)
        stb_scr[...] = jnp.zeros_like(stb_scr)

    def one_direction(chunk, gate_ref, lb_ref, st_scr, bm_ref, mk_ref, o_scr, total_row):
        rows = pl.ds(pl.multiple_of(chunk * c, c), c)
        lb = lb_ref[...]
        f = lb + (1.0 - lb) * jax.nn.sigmoid(gate_ref[rows, :])
        o_scr[rows, :] = _chunk_step(q_ref[rows, :] * scale, 1.0 - f, i_ref[rows, :].astype(BF16),
                                     -jnp.log(f), st_scr, bm_ref, mk_ref, c, nlev, total_row)

    def body(ci, carry):
        one_direction(ci, ff_ref, lbf_ref, stf_scr, bmf_ref, mkf_ref, of_scr, c - 1)
        one_direction(nchunks - 1 - ci, fb_ref, lbb_ref, stb_scr, bmb_ref, mkb_ref, ob_scr, 0)
        return carry

    lax.fori_loop(0, nchunks, body, 0)
    _head_norm_out(of_scr, ob_scr, g_ref, nw_ref, out_ref)
    sf_ref[...] = stf_scr[...].T
    sb_ref[...] = stb_scr[...].T


def _hgrn(proj, lb_f, lb_b, norm_w, s0_f, s0_b, consts, *, batch, seq, row_block0, heads, dk):
    bmf, bmb, mkf, mkb, nlev = consts
    c = REC_CHUNK
    has_s0 = s0_f is not None

    def col(group):
        return pl.BlockSpec((seq, dk), lambda b, h: (row_block0 + b, group * heads + h))

    def const(arr):
        return pl.BlockSpec(arr.shape, lambda b, h: (0,) * arr.ndim)

    state_spec = pl.BlockSpec((None, None, dk, dk), lambda b, h: (b, h, 0, 0))
    per_head = pl.BlockSpec((None, 1, dk), lambda b, h: (h, 0, 0))
    in_specs = [col(0), col(1), col(2), col(3), col(4), per_head, per_head,
                pl.BlockSpec((1, dk), lambda b, h: (0, 0)),
                const(bmf), const(bmb), const(mkf), const(mkb)]
    args = [proj, proj, proj, proj, proj, lb_f.reshape(heads, 1, dk), lb_b.reshape(heads, 1, dk),
            norm_w.reshape(1, dk), bmf, bmb, mkf, mkb]
    if has_s0:
        in_specs += [state_spec, state_spec]
        args += [s0_f, s0_b]
    return pl.pallas_call(
        functools.partial(_hgrn_kernel, c=c, nchunks=seq // c, nlev=nlev, scale=dk ** -0.5, has_s0=has_s0),
        grid=(batch, heads),
        in_specs=in_specs,
        out_specs=[pl.BlockSpec((seq, dk), lambda b, h: (b, h)), state_spec, state_spec],
        out_shape=[jax.ShapeDtypeStruct((batch * seq, heads * dk), BF16),
                   jax.ShapeDtypeStruct((batch, heads, dk, dk), F32),
                   jax.ShapeDtypeStruct((batch, heads, dk, dk), F32)],
        scratch_shapes=[pltpu.VMEM((seq, dk), F32), pltpu.VMEM((seq, dk), F32),
                        pltpu.VMEM((dk, dk), F32), pltpu.VMEM((dk, dk), F32)],
        compiler_params=_cparams("arbitrary", "arbitrary"),
        name="hgrn_scan",
    )(*args)


def _gla_kernel(*refs, c, nchunks, nlev, scale, has_s0):
    (q_ref, k_ref, v_ref, g_ref, lr_ref, ufh_ref, ufl_ref, ubh_ref, ubl_ref, bf_ref, bb_ref, nw_ref,
     bmf_ref, bmb_ref, mkf_ref, mkb_ref) = refs[:16]
    refs = refs[16:]
    if has_s0:
        s0f_ref, s0b_ref = refs[:2]
        refs = refs[2:]
    out_ref, sf_ref, sb_ref, of_scr, ob_scr, stf_scr, stb_scr = refs

    if has_s0:
        stf_scr[...] = s0f_ref[...].T
        stb_scr[...] = s0b_ref[...].T
    else:
        stf_scr[...] = jnp.zeros_like(stf_scr)
        stb_scr[...] = jnp.zeros_like(stb_scr)

    def one_direction(chunk, uh_ref, ul_ref, bias_ref, st_scr, bm_ref, mk_ref, o_scr, total_row):
        rows = pl.ds(pl.multiple_of(chunk * c, c), c)
        lr_hi, lr_lo = _split_hi_lo(lr_ref[rows, :])
        z = _dot(lr_hi, uh_ref[...]) + _dot(lr_hi, ul_ref[...]) + _dot(lr_lo, uh_ref[...]) + bias_ref[...]
        nl = (jnp.maximum(-z, 0.0) + jnp.log(1.0 + jnp.exp(-jnp.abs(z)))) * (1.0 / GLA_LOGIT_NORM)
        o_scr[rows, :] = _chunk_step(q_ref[rows, :] * scale, k_ref[rows, :], v_ref[rows, :].astype(BF16),
                                     nl, st_scr, bm_ref, mk_ref, c, nlev, total_row)

    def body(ci, carry):
        one_direction(ci, ufh_ref, ufl_ref, bf_ref, stf_scr, bmf_ref, mkf_ref, of_scr, c - 1)
        one_direction(nchunks - 1 - ci, ubh_ref, ubl_ref, bb_ref, stb_scr, bmb_ref, mkb_ref, ob_scr, 0)
        return carry

    lax.fori_loop(0, nchunks, body, 0)
    _head_norm_out(of_scr, ob_scr, g_ref, nw_ref, out_ref)
    sf_ref[...] = stf_scr[...].T
    sb_ref[...] = stb_scr[...].T


def _gla(proj, lr, up_f, up_b, bias_f, bias_b, norm_w, s0_f, s0_b, consts, *,
         batch, seq, row_block0, heads, dk, dv, col0):
    bmf, bmb, mkf, mkb, nlev = consts
    c = REC_CHUNK
    has_s0 = s0_f is not None
    kw = heads * dk
    q0, k0, v0, g0 = col0 // dk, (col0 + kw) // dk, (col0 + 2 * kw) // dv, (col0 + 2 * kw + heads * dv) // dv

    def const(arr):
        return pl.BlockSpec(arr.shape, lambda b, h: (0,) * arr.ndim)

    def head_cols(arr):
        return pl.BlockSpec((arr.shape[0], dk), lambda b, h: (0, h))

    state_spec = pl.BlockSpec((None, None, dk, dv), lambda b, h: (b, h, 0, 0))
    ufh, ufl = _split_hi_lo(up_f)
    ubh, ubl = _split_hi_lo(up_b)
    in_specs = [pl.BlockSpec((seq, dk), lambda b, h: (row_block0 + b, q0 + h)),
                pl.BlockSpec((seq, dk), lambda b, h: (row_block0 + b, k0 + h)),
                pl.BlockSpec((seq, dv), lambda b, h: (row_block0 + b, v0 + h)),
                pl.BlockSpec((seq, dv), lambda b, h: (row_block0 + b, g0 + h)),
                pl.BlockSpec((seq, LANES), lambda b, h: (row_block0 + b, 0)),
                head_cols(ufh), head_cols(ufl), head_cols(ubh), head_cols(ubl),
                head_cols(bias_f), head_cols(bias_b),
                pl.BlockSpec((1, dv), lambda b, h: (0, 0)),
                const(bmf), const(bmb), const(mkf), const(mkb)]
    args = [proj, proj, proj, proj, lr, ufh, ufl, ubh, ubl, bias_f, bias_b, norm_w.reshape(1, dv),
            bmf, bmb, mkf, mkb]
    if has_s0:
        in_specs += [state_spec, state_spec]
        args += [s0_f, s0_b]
    return pl.pallas_call(
        functools.partial(_gla_kernel, c=c, nchunks=seq // c, nlev=nlev, scale=dk ** -0.5, has_s0=has_s0),
        grid=(batch, heads),
        in_specs=in_specs,
        out_specs=[pl.BlockSpec((seq, dv), lambda b, h: (b, h)), state_spec, state_spec],
        out_shape=[jax.ShapeDtypeStruct((batch * seq, heads * dv), BF16),
                   jax.ShapeDtypeStruct((batch, heads, dk, dv), F32),
                   jax.ShapeDtypeStruct((batch, heads, dk, dv), F32)],
        scratch_shapes=[pltpu.VMEM((seq, dv), F32), pltpu.VMEM((seq, dv), F32),
                        pltpu.VMEM((dv, dk), F32), pltpu.VMEM((dv, dk), F32)],
        compiler_params=_cparams("arbitrary", "arbitrary"),
        name="gla_scan",
    )(*args)


def _first_argmax(x, row, n):
    m = jnp.max(x, axis=0, keepdims=True)
    idx = jnp.min(jnp.where(x == m, row, float(n)), axis=0, keepdims=True)
    return m, idx


def _router_kernel(h_ref, whi_ref, wlo_ref, bias_ref, tri_ref, eidx_ref, wts_ref, rank_ref, cnt_ref, seen):
    @pl.when(pl.program_id(0) == 0)
    def _():
        seen[...] = jnp.zeros_like(seen)

    a_hi, a_lo = _split_hi_lo(h_ref[...])
    z = _dot_nt(whi_ref[...], a_hi) + _dot_nt(wlo_ref[...], a_hi) + _dot_nt(whi_ref[...], a_lo)
    scores = jax.nn.sigmoid(z)
    biased = scores + bias_ref[...]
    n_e, tm = scores.shape
    gs = n_e // N_GROUP
    neg = -jnp.inf
    row = lax.broadcasted_iota(jnp.int32, (n_e, tm), 0).astype(F32)

    group_scores = []
    grp_row = lax.broadcasted_iota(jnp.int32, (gs, tm), 0).astype(F32)
    for g in range(N_GROUP):
        xg, rg = biased[g * gs:(g + 1) * gs], grp_row
        m1, i1 = _first_argmax(xg, rg, n_e)
        m2 = jnp.max(jnp.where(rg == i1, neg, xg), axis=0, keepdims=True)
        group_scores.append(m1 + m2)
    grp = jnp.concatenate(group_scores, axis=0)
    grow = lax.broadcasted_iota(jnp.int32, (N_GROUP, tm), 0).astype(F32)
    keep = jnp.zeros((N_GROUP, tm), F32)
    for _ in range(TOPK_GROUP):
        _, gi = _first_argmax(grp, grow, N_GROUP)
        hit = grow == gi
        keep = jnp.where(hit, 1.0, keep)
        grp = jnp.where(hit, neg, grp)
    emask = jnp.concatenate([jnp.broadcast_to(keep[g:g + 1], (gs, tm)) for g in range(N_GROUP)], axis=0)
    cand = jnp.where(emask > 0.0, biased, neg)

    picked = jnp.zeros((n_e, tm), F32)
    eidx, wts = [], []
    for _ in range(TOP_K):
        _, ei = _first_argmax(cand, row, n_e)
        hit = row == ei
        eidx.append(ei)
        wts.append(jnp.sum(jnp.where(hit, scores, 0.0), axis=0, keepdims=True))
        picked = jnp.where(hit, 1.0, picked)
        cand = jnp.where(hit, neg, cand)
    total = wts[0]
    for w in wts[1:]:
        total = total + w
    before = _dot(picked.astype(BF16), tri_ref[...]) + seen[...]
    ranks = [jnp.sum(jnp.where(row == ei, before, 0.0), axis=0, keepdims=True) for ei in eidx]
    seen[...] = seen[...] + jnp.sum(picked, axis=1, keepdims=True)
    eidx_ref[...] = jnp.concatenate(eidx, axis=0).astype(jnp.int32)
    wts_ref[...] = jnp.concatenate([w / total * ROUTED_SCALE for w in wts], axis=0)
    rank_ref[...] = jnp.concatenate(ranks, axis=0).astype(jnp.int32)
    cnt_ref[...] = seen[...].astype(jnp.int32)


def _router(h, router_w, router_bias, tm):
    t, d = h.shape
    n_e = router_w.shape[1]
    w_hi, w_lo = _split_hi_lo(router_w.T)
    i = np.arange(tm)
    tri = jnp.asarray((i[:, None] < i[None, :]).astype(np.float32), BF16)
    tok_spec = pl.BlockSpec((TOP_K, tm), lambda i: (0, i))
    return pl.pallas_call(
        _router_kernel,
        grid=(t // tm,),
        in_specs=[pl.BlockSpec((tm, d), lambda i: (i, 0)),
                  pl.BlockSpec((n_e, d), lambda i: (0, 0)),
                  pl.BlockSpec((n_e, d), lambda i: (0, 0)),
                  pl.BlockSpec((n_e, 1), lambda i: (0, 0)),
                  pl.BlockSpec((tm, tm), lambda i: (0, 0))],
        out_specs=[tok_spec, tok_spec, tok_spec, pl.BlockSpec((n_e, 1), lambda i: (0, 0))],
        out_shape=[jax.ShapeDtypeStruct((TOP_K, t), jnp.int32), jax.ShapeDtypeStruct((TOP_K, t), F32),
                   jax.ShapeDtypeStruct((TOP_K, t), jnp.int32), jax.ShapeDtypeStruct((n_e, 1), jnp.int32)],
        scratch_shapes=[pltpu.VMEM((n_e, 1), F32)],
        compiler_params=_cparams("arbitrary"),
        name="router",
    )(h, w_hi, w_lo, router_bias.reshape(n_e, 1).astype(F32), tri)


def _slot_kernel(eidx_ref, rank_ref, seg_ref, slot_ref):
    n_e = seg_ref.shape[0]
    tm = eidx_ref.shape[1]
    row = lax.broadcasted_iota(jnp.int32, (n_e, tm), 0)
    seg = seg_ref[...]
    slots = []
    for k in range(eidx_ref.shape[0]):
        start = jnp.sum(jnp.where(row == eidx_ref[k:k + 1, :], seg, 0.0), axis=0, keepdims=True)
        slots.append(start.astype(jnp.int32) + rank_ref[k:k + 1, :])
    slot_ref[...] = jnp.concatenate(slots, axis=0)


def _assignment_slots(eidx_t, rank_t, seg_start, tm):
    k, t = eidx_t.shape
    n_e = seg_start.shape[0]
    tok_spec = pl.BlockSpec((k, tm), lambda i: (0, i))
    return pl.pallas_call(
        _slot_kernel,
        grid=(t // tm,),
        in_specs=[tok_spec, tok_spec, pl.BlockSpec((n_e, 1), lambda i: (0, 0))],
        out_specs=tok_spec,
        out_shape=jax.ShapeDtypeStruct((k, t), jnp.int32),
        compiler_params=_cparams("arbitrary"),
        name="assignment_slots",
    )(eidx_t, rank_t, seg_start.reshape(n_e, 1).astype(F32))


def _expert_layout(counts, n_assign, unit_rows):
    n_e = counts.shape[0]
    padded = (counts + ROW_ALIGN - 1) // ROW_ALIGN * ROW_ALIGN
    seg_end = jnp.cumsum(padded)
    seg_start = seg_end - padded
    n_slots = n_assign + n_e * ROW_ALIGN
    units_per_e = (counts + unit_rows - 1) // unit_rows
    unit_end = jnp.cumsum(units_per_e)
    unit_first = unit_end - units_per_e
    n_units = unit_end[-1]
    max_units = n_e + n_assign // unit_rows
    uid = jnp.arange(max_units, dtype=jnp.int32)
    ue = jnp.minimum(jnp.sum(uid[:, None] >= unit_end[None, :], axis=1), n_e - 1).astype(jnp.int32)
    j = uid - unit_first[ue]
    valid = uid < n_units
    last_e = ue[jnp.maximum(n_units - 1, 0)]
    unit_e = jnp.where(valid, ue, last_e)
    unit_start = jnp.where(valid, seg_start[ue] + j * unit_rows, 0)
    unit_nrows = jnp.where(valid, jnp.clip(padded[ue] - j * unit_rows, 0, unit_rows), 0)
    n_units_tail = jnp.stack([n_units, seg_end[-1]]).astype(jnp.int32)
    pad_group = jnp.where(padded > 0, seg_end - ROW_ALIGN, -1).astype(jnp.int32)
    return (seg_start.astype(jnp.int32), pad_group, unit_e.astype(jnp.int32), unit_start.astype(jnp.int32),
            unit_nrows.astype(jnp.int32), n_units_tail, n_slots, max_units)


def _dispatch_kernel(slot_ref, pad_ref, tail_ref, x_ref, xs_hbm, zrows, zsem, sem, *, tm, k, t):
    i = pl.program_id(0)

    @pl.when(i == 0)
    def _():
        zrows[...] = jnp.zeros_like(zrows)
        tail0 = pl.multiple_of(tail_ref[1], ROW_ALIGN)
        tail_groups = (xs_hbm.shape[0] - tail0) // ROW_ALIGN
        n_e = pad_ref.shape[0]

        def zero_group(r0):
            return pltpu.make_async_copy(zrows, xs_hbm.at[pl.ds(pl.multiple_of(r0, ROW_ALIGN), ROW_ALIGN)], zsem)

        def pad_start(e, carry):
            @pl.when(pad_ref[e] >= 0)
            def _():
                zero_group(pad_ref[e]).start()
            return carry

        def pad_wait(e, carry):
            @pl.when(pad_ref[e] >= 0)
            def _():
                zero_group(pad_ref[e]).wait()
            return carry

        def tail_start(g, carry):
            zero_group(tail0 + g * ROW_ALIGN).start()
            return carry

        def tail_wait(g, carry):
            zero_group(tail0 + g * ROW_ALIGN).wait()
            return carry

        lax.fori_loop(0, n_e, pad_start, 0)
        lax.fori_loop(0, tail_groups, tail_start, 0)
        lax.fori_loop(0, n_e, pad_wait, 0)
        lax.fori_loop(0, tail_groups, tail_wait, 0)

    def issue(r, carry):
        for kk in range(k):
            s = slot_ref[kk * t + i * tm + r]
            pltpu.make_async_copy(x_ref.at[pl.ds(r, 1)], xs_hbm.at[pl.ds(s, 1)], sem).start()
        return carry

    lax.fori_loop(0, tm, issue, 0)
    for kk in range(k):
        pltpu.make_async_copy(x_ref, xs_hbm.at[pl.ds(0, tm)], sem).wait()


def _dispatch(xp, slot_flat, pad_group, n_units_tail, n_slots, tm, k):
    t, dh = xp.shape
    grid_spec = pltpu.PrefetchScalarGridSpec(
        num_scalar_prefetch=3,
        grid=(t // tm,),
        in_specs=[pl.BlockSpec((tm, dh), lambda i, *_: (i, 0))],
        out_specs=pl.BlockSpec(memory_space=pl.ANY),
        scratch_shapes=[pltpu.VMEM((ROW_ALIGN, dh), xp.dtype),
                        pltpu.SemaphoreType.DMA(()), pltpu.SemaphoreType.DMA(())])
    return pl.pallas_call(
        functools.partial(_dispatch_kernel, tm=tm, k=k, t=t),
        grid_spec=grid_spec,
        out_shape=jax.ShapeDtypeStruct((n_slots, dh), xp.dtype),
        compiler_params=_cparams("arbitrary"),
        name="moe_dispatch",
    )(slot_flat, pad_group, n_units_tail, xp)


def _moe_kernel(ue_ref, us_ref, ur_ref, nu_ref,
                xs_hbm, wg_ref, wu_ref, wd_ref, y_hbm,
                xbuf, xbf, hbuf, ybuf, zrows, wgb, wub, wdb, xsem, ysem, zsem, *, bm, nf, nd):
    u = pl.program_id(0)
    s = pl.program_id(1)
    n_units = nu_ref[0]
    tn = ybuf.shape[2]

    def for_each(count, fn):
        def body(j, carry):
            fn(j)
            return carry
        lax.fori_loop(0, count, body, 0)

    def x_group(unit, g):
        r0 = pl.multiple_of(g * ROW_ALIGN, ROW_ALIGN)
        src0 = pl.multiple_of(us_ref[unit], ROW_ALIGN) + r0
        return pltpu.make_async_copy(xs_hbm.at[pl.ds(src0, ROW_ALIGN)],
                                     xbuf.at[unit % 2, pl.ds(r0, ROW_ALIGN)], xsem.at[unit % 2])

    def y_slot(unit, n):
        return (unit * nd + n) % 2

    def y_group(unit, n, g):
        r0 = pl.multiple_of(g * ROW_ALIGN, ROW_ALIGN)
        dst0 = pl.multiple_of(us_ref[unit], ROW_ALIGN) + r0
        return pltpu.make_async_copy(ybuf.at[y_slot(unit, n), pl.ds(r0, ROW_ALIGN)],
                                     y_hbm.at[pl.ds(dst0, ROW_ALIGN), pl.ds(pl.multiple_of(n * tn, tn), tn)],
                                     ysem.at[y_slot(unit, n)])

    def groups(unit):
        return ur_ref[unit] // ROW_ALIGN

    @pl.when(jnp.logical_and(u == 0, s == 0))
    def _():
        xbuf[...] = jnp.zeros_like(xbuf)
        zrows[...] = jnp.zeros_like(zrows)
        tail0 = pl.multiple_of(nu_ref[1], ROW_ALIGN)

        def tail_copy(g):
            r0 = pl.multiple_of(tail0 + g * ROW_ALIGN, ROW_ALIGN)
            return pltpu.make_async_copy(zrows, y_hbm.at[pl.ds(r0, ROW_ALIGN)], zsem)

        tail_groups = (y_hbm.shape[0] - tail0) // ROW_ALIGN
        for_each(tail_groups, lambda g: tail_copy(g).start())
        for_each(tail_groups, lambda g: tail_copy(g).wait())

    @pl.when(u < n_units)
    def _():
        nblk = (ur_ref[u] + bm - 1) // bm

        @pl.when(s == 0)
        def _():
            @pl.when(u == 0)
            def _():
                for_each(groups(u), lambda g: x_group(u, g).start())

            for_each(groups(u), lambda g: x_group(u, g).wait())

            @pl.when(u + 1 < n_units)
            def _():
                for_each(groups(u + 1), lambda g: x_group(u + 1, g).start())

            def unpack(rb):
                r0 = pl.multiple_of(rb * bm, bm)
                xbf[pl.ds(r0, bm), :] = _unpack_bf16_pairs(xbuf[u % 2, pl.ds(r0, bm), :])

            for_each(nblk, unpack)

        @pl.when(s < nf)
        def _():
            wgb[...] = wg_ref[...].astype(BF16)
            wub[...] = wu_ref[...].astype(BF16)

            def block(rb):
                r0 = pl.multiple_of(rb * bm, bm)
                xb = xbf[pl.ds(r0, bm), :]
                g = _dot(xb, wgb[...])
                up = _dot(xb, wub[...])
                hbuf[s, pl.ds(r0, bm), :] = (_silu(g) * up).astype(BF16)

            for_each(nblk, block)

        @pl.when(s >= nf)
        def _():
            n = s - nf
            wdb[...] = wd_ref[...].astype(BF16)

            def block(rb):
                r0 = pl.multiple_of(rb * bm, bm)
                h = jnp.concatenate([hbuf[j, pl.ds(r0, bm), :] for j in range(nf)], axis=1)
                ybuf[y_slot(u, n), pl.ds(r0, bm), :] = _dot(h, wdb[...])

            for_each(nblk, block)
            for_each(groups(u), lambda g: y_group(u, n, g).start())

            @pl.when(n > 0)
            def _():
                for_each(groups(u), lambda g: y_group(u, n - 1, g).wait())

            @pl.when(jnp.logical_and(n == 0, u > 0))
            def _():
                for_each(groups(u - 1), lambda g: y_group(u - 1, nd - 1, g).wait())

            @pl.when(jnp.logical_and(n == nd - 1, u == n_units - 1))
            def _():
                for_each(groups(u), lambda g: y_group(u, n, g).wait())


def _grouped_swiglu(xs, unit_e, unit_start, unit_nrows, n_units_tail, max_units,
                    w_gate, w_up, w_down, unit_rows, name):
    n_slots, dh = xs.shape
    d = 2 * dh
    n_e, _, de = w_gate.shape
    tf = _tile(de, MOE_F_TILE)
    nf = de // tf
    tn = _tile(d, MOE_OUT_TILE)
    nd = d // tn
    bm = min(MOE_ROW_BLOCK, unit_rows)

    def up_map(u, s, ue, us, ur, nu):
        return ue[u], 0, jnp.where(u < nu[0], jnp.minimum(s, nf - 1), nf - 1)

    def down_map(u, s, ue, us, ur, nu):
        return ue[u], 0, jnp.where(u < nu[0], jnp.maximum(s - nf, 0), nd - 1)

    grid_spec = pltpu.PrefetchScalarGridSpec(
        num_scalar_prefetch=4,
        grid=(max_units, nf + nd),
        in_specs=[pl.BlockSpec(memory_space=pl.ANY),
                  pl.BlockSpec((None, d, tf), up_map),
                  pl.BlockSpec((None, d, tf), up_map),
                  pl.BlockSpec((None, de, tn), down_map)],
        out_specs=pl.BlockSpec(memory_space=pl.ANY),
        scratch_shapes=[pltpu.VMEM((2, unit_rows, dh), jnp.uint32),
                        pltpu.VMEM((unit_rows, d), BF16),
                        pltpu.VMEM((nf, unit_rows, tf), BF16),
                        pltpu.VMEM((2, unit_rows, tn), F32),
                        pltpu.VMEM((ROW_ALIGN, d), F32),
                        pltpu.VMEM((d, tf), BF16), pltpu.VMEM((d, tf), BF16), pltpu.VMEM((de, tn), BF16),
                        pltpu.SemaphoreType.DMA((2,)), pltpu.SemaphoreType.DMA((2,)),
                        pltpu.SemaphoreType.DMA(())])
    return pl.pallas_call(
        functools.partial(_moe_kernel, bm=bm, nf=nf, nd=nd),
        grid_spec=grid_spec,
        out_shape=jax.ShapeDtypeStruct((n_slots, d), F32),
        compiler_params=_cparams("arbitrary", "arbitrary"),
        name=name,
    )(unit_e, unit_start, unit_nrows, n_units_tail, xs, w_gate, w_up, w_down)


def _combine_kernel(pos_ref, ys_hbm, wts_ref, ysh_ref, x1_ref, mod_ref, fw_ref, out_ref, buf, sem, *, tm, k, t):
    i = pl.program_id(0)
    n = pl.num_programs(0)

    def gather(step):
        def issue(r, carry):
            for kk in range(k):
                p = pos_ref[kk * t + step * tm + r]
                pltpu.make_async_copy(ys_hbm.at[pl.ds(p, 1)], buf.at[step % 2, kk, pl.ds(r, 1)],
                                      sem.at[step % 2]).start()
            return carry

        lax.fori_loop(0, tm, issue, 0)

    @pl.when(i == 0)
    def _():
        gather(i)

    @pl.when(i + 1 < n)
    def _():
        gather(i + 1)

    for kk in range(k):
        pltpu.make_async_copy(ys_hbm.at[pl.ds(0, tm)], buf.at[i % 2, kk], sem.at[i % 2]).wait()
    acc = buf[i % 2, 0] * wts_ref[:, 0:1]
    for kk in range(1, k):
        acc = acc + buf[i % 2, kk] * wts_ref[:, kk:kk + 1]
    ff = acc + ysh_ref[...]
    x2 = x1_ref[...] + mod_ref[0, 5:6, :] * ff
    out_ref[...] = x2 * lax.rsqrt(jnp.mean(x2 * x2, axis=-1, keepdims=True) + EPS) * fw_ref[...]


def _combine(y_sorted, assign_slot, wts, y_shared, x1, mod3, final_w, tm, ctx_rows, lat_seq):
    t, d = x1.shape
    k = wts.shape[1]
    cb = ctx_rows // tm
    bps = lat_seq // tm
    ctx_row = (t - ctx_rows) // lat_seq
    grid_spec = pltpu.PrefetchScalarGridSpec(
        num_scalar_prefetch=1,
        grid=(t // tm,),
        in_specs=[pl.BlockSpec(memory_space=pl.ANY),
                  pl.BlockSpec((tm, k), lambda i, pos: (i, 0)),
                  pl.BlockSpec((tm, d), lambda i, pos: (i, 0)),
                  pl.BlockSpec((tm, d), lambda i, pos: (i, 0)),
                  pl.BlockSpec((1, 6, d), _mod_row_map(cb, bps, ctx_row)),
                  pl.BlockSpec((1, d), lambda i, pos: (0, 0))],
        out_specs=pl.BlockSpec((tm, d), lambda i, pos: (i, 0)),
        scratch_shapes=[pltpu.VMEM((2, k, tm, d), F32), pltpu.SemaphoreType.DMA((2,))])
    return pl.pallas_call(
        functools.partial(_combine_kernel, tm=tm, k=k, t=t),
        grid_spec=grid_spec,
        out_shape=jax.ShapeDtypeStruct((t, d), F32),
        compiler_params=_cparams("arbitrary"),
        name="moe_combine",
    )(assign_slot, y_sorted, wts, y_shared, x1, mod3, final_w)


def _grid_pos_embed(n_tokens, d_model):
    rows = n_tokens // GRID_W
    r, col = jnp.meshgrid(jnp.arange(rows, dtype=F32), jnp.arange(GRID_W, dtype=F32), indexing="ij")
    n_freq = d_model // 4
    omega = 1.0 / (10000.0 ** (jnp.arange(n_freq, dtype=F32) / n_freq))

    def enc(p):
        ang = p.reshape(-1)[:, None] * omega[None, :]
        return jnp.concatenate([jnp.sin(ang), jnp.cos(ang)], axis=-1)

    return jnp.concatenate([enc(r), enc(col)], axis=-1)


def _pad_rows(a, rows, row0=0):
    return jnp.zeros((rows,) + a.shape[1:], a.dtype).at[row0:row0 + a.shape[0]].set(a)


def kernel(x_prompt, x_sample, state_hgrn_fwd, state_hgrn_bwd, state_gla_fwd, state_gla_bwd, c, c_ctx, norm1_w, norm2_w, ada_w, ada_b, w_in, hg_lb_logits, gla_gk_up_fwd, gla_gk_bias_fwd, gla_gk_up_bwd, gla_gk_bias_bwd, hg_norm_w, gla_norm_w, w_out, router_w, router_bias, exp_w_gate, exp_w_up, exp_w_down, sh_w_gate, sh_w_up, sh_w_down, final_norm_w):
    n_ctx, ctx_seq, d = x_prompt.shape
    n_lat, lat_seq, _ = x_sample.shape
    depth = norm1_w.shape[0]
    assert depth == 1, "one trunk layer"
    _, _, hg_heads, hg_dk, hg_dv = state_hgrn_fwd.shape
    _, _, gla_heads, gla_dk, gla_dv = state_gla_fwd.shape
    assert hg_dk == hg_dv
    rank = gla_gk_up_fwd.shape[1]
    n_experts = exp_w_gate.shape[1]
    hg_w = hg_heads * hg_dk
    gla_kw = gla_heads * gla_dk
    gla_w = gla_heads * gla_dv
    tp, ts = n_ctx * ctx_seq, n_lat * lat_seq
    t = tp + ts
    assert tp % lat_seq == 0 and 2 * rank <= LANES and n_lat < 8
    tm = min(256, ctx_seq)
    layer = 0

    lb_all = jnp.cumsum(jax.nn.softmax(hg_lb_logits.astype(F32), axis=1), axis=1)
    pos = _grid_pos_embed(lat_seq, d)

    c_rows = jnp.zeros((8, d), F32).at[:n_lat].set(c).at[n_lat].set(c_ctx)
    mod = _ada(c_rows, ada_w[layer], ada_b[layer].reshape(1, -1))
    mod3 = mod.reshape(8, 6, d)

    x0, h1 = _norm1(x_prompt.reshape(tp, d), x_sample.reshape(ts, d), pos, mod3,
                    norm1_w[layer].reshape(1, d), tm, lat_seq)

    lr0 = 5 * hg_w + 2 * gla_kw + gla_w
    w_l = w_in[layer]
    w_main = jnp.concatenate([w_l[:, :lr0], w_l[:, lr0 + 2 * rank:]], axis=1).astype(BF16)
    w_lr = jnp.zeros((d, LANES), BF16).at[:, :2 * rank].set(w_l[:, lr0:lr0 + 2 * rank].astype(BF16))
    proj = _matmul(h1, w_main, 1024, 1024, "in_proj")
    lr = _matmul(h1, w_lr, 1024, LANES, "in_proj_lr")

    consts = _level_constants(REC_CHUNK)
    up_f = _pad_rows(gla_gk_up_fwd[layer], LANES, 0)
    up_b = _pad_rows(gla_gk_up_bwd[layer], LANES, rank)
    bias_f = gla_gk_bias_fwd[layer].reshape(1, gla_kw)
    bias_b = gla_gk_bias_bwd[layer].reshape(1, gla_kw)
    hg_args = dict(heads=hg_heads, dk=hg_dk)
    gla_args = dict(heads=gla_heads, dk=gla_dk, dv=gla_dv, col0=5 * hg_w)

    hg_ctx, new_hf, new_hb = _hgrn(proj, lb_all[0, layer], lb_all[1, layer], hg_norm_w[layer], None, None, consts,
                                   batch=n_ctx, seq=ctx_seq, row_block0=0, **hg_args)
    hg_lat, _, _ = _hgrn(proj, lb_all[0, layer], lb_all[1, layer], hg_norm_w[layer],
                         state_hgrn_fwd[:, layer], state_hgrn_bwd[:, layer], consts,
                         batch=n_lat, seq=lat_seq, row_block0=tp // lat_seq, **hg_args)
    gla_ctx, new_gf, new_gb = _gla(proj, lr, up_f, up_b, bias_f, bias_b, gla_norm_w[layer], None, None, consts,
                                   batch=n_ctx, seq=ctx_seq, row_block0=0, **gla_args)
    gla_lat, _, _ = _gla(proj, lr, up_f, up_b, bias_f, bias_b, gla_norm_w[layer],
                         state_gla_fwd[:, layer], state_gla_bwd[:, layer], consts,
                         batch=n_lat, seq=lat_seq, row_block0=tp // lat_seq, **gla_args)

    w_o = w_out[layer].astype(BF16)
    mix_ctx = _matmul2(hg_ctx, w_o[:hg_w], gla_ctx, w_o[hg_w:], 1024, 1024, "out_proj_ctx")
    mix_lat = _matmul2(hg_lat, w_o[:hg_w], gla_lat, w_o[hg_w:], 1024, 1024, "out_proj_lat")

    x1, h2, h2p = _norm2(x0, mix_ctx, mix_lat, mod3, norm2_w[layer].reshape(1, d), tm, lat_seq)

    tok_tile = _tile(t, 512)
    eidx_t, wts_t, rank_t, counts = _router(h2, router_w[layer], router_bias[layer], tok_tile)
    seg_start, pad_group, unit_e, unit_start, unit_nrows, n_units_tail, n_slots, max_units = _expert_layout(
        counts.reshape(n_experts), t * TOP_K, MOE_UNIT_ROWS)
    slot_flat = _assignment_slots(eidx_t, rank_t, seg_start, tok_tile).reshape(TOP_K * t)
    xs = _dispatch(h2p, slot_flat, pad_group, n_units_tail, n_slots, tm, TOP_K)
    y_sorted = _grouped_swiglu(xs, unit_e, unit_start, unit_nrows, n_units_tail, max_units,
                               exp_w_gate[layer], exp_w_up[layer], exp_w_down[layer], MOE_UNIT_ROWS, "moe_experts")

    assert t % MOE_UNIT_ROWS == 0
    sh_units = t // MOE_UNIT_ROWS
    sh_uid = jnp.arange(sh_units, dtype=jnp.int32)
    y_shared = _grouped_swiglu(h2p, jnp.zeros((sh_units,), jnp.int32), sh_uid * MOE_UNIT_ROWS,
                               jnp.full((sh_units,), MOE_UNIT_ROWS, jnp.int32),
                               jnp.array([sh_units, t], jnp.int32), sh_units,
                               sh_w_gate[layer][None], sh_w_up[layer][None], sh_w_down[layer][None],
                               MOE_UNIT_ROWS, "shared_expert")

    y = _combine(y_sorted, slot_flat, wts_t.T, y_shared, x1, mod3, final_norm_w.reshape(1, d), 64, tp, lat_seq)
    y_prompt = y[:tp].reshape(n_ctx, ctx_seq, d)
    y_sample = y[tp:].reshape(n_lat, lat_seq, d)
    return (y_prompt, y_sample, new_hf[:, None], new_hb[:, None], new_gf[:, None], new_gb[:, None])
```

```python
import functools
import math

import numpy as np
import jax
import jax.numpy as jnp
from jax import lax
from jax.experimental import pallas as pl
from jax.experimental.pallas import tpu as pltpu

F32 = jnp.float32
BF16 = jnp.bfloat16

GRID_W = 64
TOP_K = 8
N_GROUP = 8
TOPK_GROUP = 4
ROUTED_SCALE = 2.5
GLA_LOGIT_NORM = 16.0
EPS = 1e-6

V7X_VMEM_LIMIT_BYTES = 56 * 1024 * 1024
LANES = 128
ROW_ALIGN = 16
HGRN_CTX_CHUNK, HGRN_LAT_CHUNK, GLA_CTX_CHUNK, GLA_LAT_CHUNK = 128, 128, 128, 256
MOE_UNIT_ROWS = 512
MOE_F_TILE = 256
MOE_OUT_TILE = 1024


def _cparams(*sem):
    return pltpu.CompilerParams(dimension_semantics=sem, vmem_limit_bytes=V7X_VMEM_LIMIT_BYTES)


def _tile(n, preferred):
    t = min(preferred, n)
    while n % t:
        t //= 2
    return t


def _dot(a, b):
    return jnp.dot(a, b, preferred_element_type=F32)


def _dot_nt(a, b):
    return lax.dot_general(a, b, (((1,), (1,)), ((), ())), preferred_element_type=F32)


def _dot_tn(a, b):
    return lax.dot_general(a, b, (((0,), (0,)), ((), ())), preferred_element_type=F32)


def _silu(x):
    return x * jax.nn.sigmoid(x)


def _split_hi_lo(x):
    hi = x.astype(BF16)
    return hi, (x - hi.astype(F32)).astype(BF16)


def _ada_kernel(c_ref, w_ref, b_ref, o_ref):
    a = _silu(c_ref[...]).astype(BF16)
    o_ref[...] = _dot(a, w_ref[...].astype(BF16)) + b_ref[...]


def _ada(c_rows, ada_w, ada_b):
    d, n = ada_w.shape
    rows = c_rows.shape[0]
    tn = min(512, n)
    return pl.pallas_call(
        _ada_kernel,
        grid=(n // tn,),
        in_specs=[pl.BlockSpec((rows, d), lambda j: (0, 0)),
                  pl.BlockSpec((d, tn), lambda j: (0, j)),
                  pl.BlockSpec((1, tn), lambda j: (0, j))],
        out_specs=pl.BlockSpec((rows, tn), lambda j: (0, j)),
        out_shape=jax.ShapeDtypeStruct((rows, n), F32),
        compiler_params=_cparams("arbitrary"),
        name="ada_mod",
    )(c_rows, ada_w, ada_b)


def _modulated_norm(x, w, shift, scale):
    y = x * lax.rsqrt(jnp.mean(x * x, axis=-1, keepdims=True) + EPS) * w
    return y * (1.0 + scale) + shift


def _norm1_kernel(xp_ref, xs_ref, pos_ref, mod_ref, w_ref, x0_ref, h_ref, *, ctx_blocks):
    i = pl.program_id(0)

    def finish(x):
        x0_ref[...] = x
        h_ref[...] = _modulated_norm(x, w_ref[...], mod_ref[0, 0:1, :], mod_ref[0, 1:2, :]).astype(BF16)

    @pl.when(i < ctx_blocks)
    def _():
        finish(xp_ref[...])

    @pl.when(i >= ctx_blocks)
    def _():
        finish(xs_ref[...] + pos_ref[...])


def _mod_row_map(ctx_blocks, blocks_per_seq, ctx_row):
    return lambda i, *_: (jnp.where(i < ctx_blocks, ctx_row, (i - ctx_blocks) // blocks_per_seq), 0, 0)


def _norm1(xp, xs, pos, mod3, w, tm, lat_seq):
    tp, d = xp.shape
    ts = xs.shape[0]
    t = tp + ts
    cb = tp // tm
    bps = lat_seq // tm
    ctx_row = ts // lat_seq
    return pl.pallas_call(
        functools.partial(_norm1_kernel, ctx_blocks=cb),
        grid=(t // tm,),
        in_specs=[pl.BlockSpec((tm, d), lambda i: (jnp.minimum(i, cb - 1), 0)),
                  pl.BlockSpec((tm, d), lambda i: (jnp.maximum(i - cb, 0), 0)),
                  pl.BlockSpec((tm, d), lambda i: (jnp.maximum(i - cb, 0) % bps, 0)),
                  pl.BlockSpec((1, 6, d), _mod_row_map(cb, bps, ctx_row)),
                  pl.BlockSpec((1, d), lambda i: (0, 0))],
        out_specs=[pl.BlockSpec((tm, d), lambda i: (i, 0)),
                   pl.BlockSpec((tm, d), lambda i: (i, 0))],
        out_shape=[jax.ShapeDtypeStruct((t, d), F32), jax.ShapeDtypeStruct((t, d), BF16)],
        compiler_params=_cparams("arbitrary"),
        name="norm1",
    )(xp, xs, pos, mod3, w)


def _pack_bf16_pairs(h):
    half = h.shape[1] // 2
    bits = pltpu.bitcast(h.astype(BF16).astype(F32), jnp.uint32)
    return (bits[:, :half] >> 16) | (bits[:, half:] & jnp.uint32(0xFFFF0000))


def _unpack_bf16_pairs(w):
    lo = pltpu.bitcast(w << 16, F32).astype(BF16)
    hi = pltpu.bitcast(w & jnp.uint32(0xFFFF0000), F32).astype(BF16)
    return jnp.concatenate([lo, hi], axis=1)


def _norm2_kernel(x0_ref, mc_ref, ml_ref, mod_ref, w_ref, x1_ref, h_ref, hp_ref, *, ctx_blocks):
    i = pl.program_id(0)

    def finish(mix):
        x1 = x0_ref[...] + mod_ref[0, 2:3, :] * mix
        x1_ref[...] = x1
        h = _modulated_norm(x1, w_ref[...], mod_ref[0, 3:4, :], mod_ref[0, 4:5, :])
        h_ref[...] = h
        hp_ref[...] = _pack_bf16_pairs(h)

    @pl.when(i < ctx_blocks)
    def _():
        finish(mc_ref[...])

    @pl.when(i >= ctx_blocks)
    def _():
        finish(ml_ref[...])


def _norm2(x0, mix_ctx, mix_lat, mod3, w, tm, lat_seq):
    t, d = x0.shape
    tp = mix_ctx.shape[0]
    cb = tp // tm
    bps = lat_seq // tm
    ctx_row = (t - tp) // lat_seq
    return pl.pallas_call(
        functools.partial(_norm2_kernel, ctx_blocks=cb),
        grid=(t // tm,),
        in_specs=[pl.BlockSpec((tm, d), lambda i: (i, 0)),
                  pl.BlockSpec((tm, d), lambda i: (jnp.minimum(i, cb - 1), 0)),
                  pl.BlockSpec((tm, d), lambda i: (jnp.maximum(i - cb, 0), 0)),
                  pl.BlockSpec((1, 6, d), _mod_row_map(cb, bps, ctx_row)),
                  pl.BlockSpec((1, d), lambda i: (0, 0))],
        out_specs=[pl.BlockSpec((tm, d), lambda i: (i, 0)),
                   pl.BlockSpec((tm, d), lambda i: (i, 0)),
                   pl.BlockSpec((tm, d // 2), lambda i: (i, 0))],
        out_shape=[jax.ShapeDtypeStruct((t, d), F32), jax.ShapeDtypeStruct((t, d), F32),
                   jax.ShapeDtypeStruct((t, d // 2), jnp.uint32)],
        compiler_params=_cparams("arbitrary"),
        name="norm2",
    )(x0, mix_ctx, mix_lat, mod3, w)


def _mm_kernel(a_ref, b_ref, o_ref):
    o_ref[...] = _dot(a_ref[...], b_ref[...])


def _matmul(a, b, tm, tn, name):
    m, k = a.shape
    n = b.shape[1]
    tm, tn = _tile(m, tm), _tile(n, tn)
    return pl.pallas_call(
        _mm_kernel,
        grid=(n // tn, m // tm),
        in_specs=[pl.BlockSpec((tm, k), lambda j, i: (i, 0)),
                  pl.BlockSpec((k, tn), lambda j, i: (0, j))],
        out_specs=pl.BlockSpec((tm, tn), lambda j, i: (i, j)),
        out_shape=jax.ShapeDtypeStruct((m, n), F32),
        compiler_params=_cparams("arbitrary", "arbitrary"),
        name=name,
    )(a, b)


def _mm2_kernel(a1_ref, b1_ref, a2_ref, b2_ref, o_ref):
    o_ref[...] = _dot(a1_ref[...], b1_ref[...]) + _dot(a2_ref[...], b2_ref[...])


def _matmul2(a1, b1, a2, b2, tm, tn, name):
    m, k1 = a1.shape
    k2 = a2.shape[1]
    n = b1.shape[1]
    tm, tn = _tile(m, tm), _tile(n, tn)
    return pl.pallas_call(
        _mm2_kernel,
        grid=(n // tn, m // tm),
        in_specs=[pl.BlockSpec((tm, k1), lambda j, i: (i, 0)),
                  pl.BlockSpec((k1, tn), lambda j, i: (0, j)),
                  pl.BlockSpec((tm, k2), lambda j, i: (i, 0)),
                  pl.BlockSpec((k2, tn), lambda j, i: (0, j))],
        out_specs=pl.BlockSpec((tm, tn), lambda j, i: (i, j)),
        out_shape=jax.ShapeDtypeStruct((m, n), F32),
        compiler_params=_cparams("arbitrary", "arbitrary"),
        name=name,
    )(a1, b1, a2, b2)


def _level_constants(c):
    nlev = int(math.log2(c))
    i = np.arange(c)[:, None]
    t = np.arange(c)[None, :]
    groups = [t <= i, t > i]
    masks = [i == t]
    for lev in range(nlev):
        s = 1 << lev
        base = (i // (2 * s)) * (2 * s)
        pivot = base + s - 1
        right = i >= base + s
        groups.append(np.where(right, (t > pivot) & (t <= i), (t > i) & (t <= pivot)))
        same = (i // (2 * s)) == (t // (2 * s))
        masks.append(same & right & ((t % (2 * s)) < s))
    bmat_f = np.concatenate([g.astype(np.float32) for g in groups], axis=0)
    bmat_b = np.concatenate([g[::-1, ::-1].astype(np.float32) for g in groups], axis=0)
    mask_f = np.stack([m.astype(np.float32) for m in masks])
    mask_b = np.stack([m[::-1, ::-1].astype(np.float32) for m in masks])
    return (jnp.asarray(bmat_f, BF16), jnp.asarray(bmat_b, BF16),
            jnp.asarray(mask_f, F32), jnp.asarray(mask_b, F32), nlev)


def _chunk_step(q_s, k, v_bf, nl, st_ref, bmat_ref, mask_ref, c, nlev, total_row):
    n1 = nl.astype(BF16)
    r1 = nl - n1.astype(F32)
    n2 = r1.astype(BF16)
    n3 = (r1 - n2.astype(F32)).astype(BF16)
    bm = bmat_ref[...]
    dec = _dot(bm, n1) + _dot(bm, n2) + _dot(bm, n3)
    cum = dec[0:c]
    tail = dec[c:2 * c]
    st = st_ref[...]
    o = _dot_nt((q_s * jnp.exp(-cum)).astype(BF16), st.astype(BF16))
    att = mask_ref[0] * _dot_nt(q_s.astype(BF16), k.astype(BF16))
    for lev in range(nlev):
        e = jnp.exp(-dec[(2 + lev) * c:(3 + lev) * c])
        att = att + mask_ref[1 + lev] * _dot_nt((q_s * e).astype(BF16), (k * e).astype(BF16))
    o = o + _dot(att.astype(BF16), v_bf)
    k_hat = (k * jnp.exp(-tail)).astype(BF16)
    total = cum[total_row:total_row + 1]
    st_ref[...] = st * jnp.exp(-total) + _dot_tn(v_bf, k_hat)
    return o


def _head_norm_out(of_scr, ob_scr, g_ref, nw_ref, out_ref):
    o = of_scr[...] + ob_scr[...]
    y = o * lax.rsqrt(jnp.mean(o * o, axis=-1, keepdims=True) + EPS) * nw_ref[...]
    out_ref[...] = (y * _silu(g_ref[...])).astype(BF16)


def _hgrn_kernel(*refs, c, nchunks, nlev, scale, has_s0):
    (q_ref, ff_ref, fb_ref, i_ref, g_ref, lbf_ref, lbb_ref, nw_ref,
     bmf_ref, bmb_ref, mkf_ref, mkb_ref) = refs[:12]
    refs = refs[12:]
    if has_s0:
        s0f_ref, s0b_ref = refs[:2]
        refs = refs[2:]
    out_ref, sf_ref, sb_ref, of_scr, ob_scr, stf_scr, stb_scr = refs

    if has_s0:
        stf_scr[...] = s0f_ref[...].T
        stb_scr[...] = s0b_ref[...].T
    else:
        stf_scr[...] = jnp.zeros_like(stf_scr)
        stb_scr[...] = jnp.zeros_like(stb_scr)

    def one_direction(chunk, gate_ref, lb_ref, st_scr, bm_ref, mk_ref, o_scr, total_row):
        rows = pl.ds(pl.multiple_of(chunk * c, c), c)
        lb = lb_ref[...]
        f = lb + (1.0 - lb) * jax.nn.sigmoid(gate_ref[rows, :])
        o_scr[rows, :] = _chunk_step(q_ref[rows, :] * scale, 1.0 - f, i_ref[rows, :].astype(BF16),
                                     -jnp.log(f), st_scr, bm_ref, mk_ref, c, nlev, total_row)

    def body(ci, carry):
        one_direction(ci, ff_ref, lbf_ref, stf_scr, bmf_ref, mkf_ref, of_scr, c - 1)
        one_direction(nchunks - 1 - ci, fb_ref, lbb_ref, stb_scr, bmb_ref, mkb_ref, ob_scr, 0)
        return carry

    lax.fori_loop(0, nchunks, body, 0)
    _head_norm_out(of_scr, ob_scr, g_ref, nw_ref, out_ref)
    sf_ref[...] = stf_scr[...].T
    sb_ref[...] = stb_scr[...].T


def _hgrn(proj, lb_f, lb_b, norm_w, s0_f, s0_b, consts, *, batch, seq, row_block0, heads, dk):
    bmf, bmb, mkf, mkb, nlev = consts
    c = mkf.shape[1]
    has_s0 = s0_f is not None

    def col(group):
        return pl.BlockSpec((seq, dk), lambda b, h: (row_block0 + b, group * heads + h))

    def const(arr):
        return pl.BlockSpec(arr.shape, lambda b, h: (0,) * arr.ndim)

    state_spec = pl.BlockSpec((None, None, dk, dk), lambda b, h: (b, h, 0, 0))
    per_head = pl.BlockSpec((None, 1, dk), lambda b, h: (h, 0, 0))
    in_specs = [col(0), col(1), col(2), col(3), col(4), per_head, per_head,
                pl.BlockSpec((1, dk), lambda b, h: (0, 0)),
                const(bmf), const(bmb), const(mkf), const(mkb)]
    args = [proj, proj, proj, proj, proj, lb_f.reshape(heads, 1, dk), lb_b.reshape(heads, 1, dk),
            norm_w.reshape(1, dk), bmf, bmb, mkf, mkb]
    if has_s0:
        in_specs += [state_spec, state_spec]
        args += [s0_f, s0_b]
    return pl.pallas_call(
        functools.partial(_hgrn_kernel, c=c, nchunks=seq // c, nlev=nlev, scale=dk ** -0.5, has_s0=has_s0),
        grid=(batch, heads),
        in_specs=in_specs,
        out_specs=[pl.BlockSpec((seq, dk), lambda b, h: (b, h)), state_spec, state_spec],
        out_shape=[jax.ShapeDtypeStruct((batch * seq, heads * dk), BF16),
                   jax.ShapeDtypeStruct((batch, heads, dk, dk), F32),
                   jax.ShapeDtypeStruct((batch, heads, dk, dk), F32)],
        scratch_shapes=[pltpu.VMEM((seq, dk), F32), pltpu.VMEM((seq, dk), F32),
                        pltpu.VMEM((dk, dk), F32), pltpu.VMEM((dk, dk), F32)],
        compiler_params=_cparams("arbitrary", "arbitrary"),
        name="hgrn_scan",
    )(*args)


def _gla_kernel(*refs, c, nchunks, nlev, scale, has_s0):
    (q_ref, k_ref, v_ref, g_ref, lr_ref, ufh_ref, ufl_ref, ubh_ref, ubl_ref, bf_ref, bb_ref, nw_ref,
     bmf_ref, bmb_ref, mkf_ref, mkb_ref) = refs[:16]
    refs = refs[16:]
    if has_s0:
        s0f_ref, s0b_ref = refs[:2]
        refs = refs[2:]
    out_ref, sf_ref, sb_ref, of_scr, ob_scr, stf_scr, stb_scr = refs

    if has_s0:
        stf_scr[...] = s0f_ref[...].T
        stb_scr[...] = s0b_ref[...].T
    else:
        stf_scr[...] = jnp.zeros_like(stf_scr)
        stb_scr[...] = jnp.zeros_like(stb_scr)

    def one_direction(chunk, uh_ref, ul_ref, bias_ref, st_scr, bm_ref, mk_ref, o_scr, total_row):
        rows = pl.ds(pl.multiple_of(chunk * c, c), c)
        lr_hi, lr_lo = _split_hi_lo(lr_ref[rows, :])
        z = _dot(lr_hi, uh_ref[...]) + _dot(lr_hi, ul_ref[...]) + _dot(lr_lo, uh_ref[...]) + bias_ref[...]
        nl = (jnp.maximum(-z, 0.0) + jnp.log(1.0 + jnp.exp(-jnp.abs(z)))) * (1.0 / GLA_LOGIT_NORM)
        o_scr[rows, :] = _chunk_step(q_ref[rows, :] * scale, k_ref[rows, :], v_ref[rows, :].astype(BF16),
                                     nl, st_scr, bm_ref, mk_ref, c, nlev, total_row)

    def body(ci, carry):
        one_direction(ci, ufh_ref, ufl_ref, bf_ref, stf_scr, bmf_ref, mkf_ref, of_scr, c - 1)
        one_direction(nchunks - 1 - ci, ubh_ref, ubl_ref, bb_ref, stb_scr, bmb_ref, mkb_ref, ob_scr, 0)
        return carry

    lax.fori_loop(0, nchunks, body, 0)
    _head_norm_out(of_scr, ob_scr, g_ref, nw_ref, out_ref)
    sf_ref[...] = stf_scr[...].T
    sb_ref[...] = stb_scr[...].T


def _gla(proj, lr, up_f, up_b, bias_f, bias_b, norm_w, s0_f, s0_b, consts, *,
         batch, seq, row_block0, heads, dk, dv, col0):
    bmf, bmb, mkf, mkb, nlev = consts
    c = mkf.shape[1]
    has_s0 = s0_f is not None
    kw = heads * dk
    q0, k0, v0, g0 = col0 // dk, (col0 + kw) // dk, (col0 + 2 * kw) // dv, (col0 + 2 * kw + heads * dv) // dv

    def const(arr):
        return pl.BlockSpec(arr.shape, lambda b, h: (0,) * arr.ndim)

    def head_cols(arr):
        return pl.BlockSpec((arr.shape[0], dk), lambda b, h: (0, h))

    state_spec = pl.BlockSpec((None, None, dk, dv), lambda b, h: (b, h, 0, 0))
    ufh, ufl = _split_hi_lo(up_f)
    ubh, ubl = _split_hi_lo(up_b)
    in_specs = [pl.BlockSpec((seq, dk), lambda b, h: (row_block0 + b, q0 + h)),
                pl.BlockSpec((seq, dk), lambda b, h: (row_block0 + b, k0 + h)),
                pl.BlockSpec((seq, dv), lambda b, h: (row_block0 + b, v0 + h)),
                pl.BlockSpec((seq, dv), lambda b, h: (row_block0 + b, g0 + h)),
                pl.BlockSpec((seq, LANES), lambda b, h: (row_block0 + b, 0)),
                head_cols(ufh), head_cols(ufl), head_cols(ubh), head_cols(ubl),
                head_cols(bias_f), head_cols(bias_b),
                pl.BlockSpec((1, dv), lambda b, h: (0, 0)),
                const(bmf), const(bmb), const(mkf), const(mkb)]
    args = [proj, proj, proj, proj, lr, ufh, ufl, ubh, ubl, bias_f, bias_b, norm_w.reshape(1, dv),
            bmf, bmb, mkf, mkb]
    if has_s0:
        in_specs += [state_spec, state_spec]
        args += [s0_f, s0_b]
    return pl.pallas_call(
        functools.partial(_gla_kernel, c=c, nchunks=seq // c, nlev=nlev, scale=dk ** -0.5, has_s0=has_s0),
        grid=(batch, heads),
        in_specs=in_specs,
        out_specs=[pl.BlockSpec((seq, dv), lambda b, h: (b, h)), state_spec, state_spec],
        out_shape=[jax.ShapeDtypeStruct((batch * seq, heads * dv), BF16),
                   jax.ShapeDtypeStruct((batch, heads, dk, dv), F32),
                   jax.ShapeDtypeStruct((batch, heads, dk, dv), F32)],
        scratch_shapes=[pltpu.VMEM((seq, dv), F32), pltpu.VMEM((seq, dv), F32),
                        pltpu.VMEM((dv, dk), F32), pltpu.VMEM((dv, dk), F32)],
        compiler_params=_cparams("arbitrary", "arbitrary"),
        name="gla_scan",
    )(*args)


def _first_argmax(x, row, n):
    m = jnp.max(x, axis=0, keepdims=True)
    idx = jnp.min(jnp.where(x == m, row, float(n)), axis=0, keepdims=True)
    return m, idx


def _router_kernel(h_ref, whi_ref, wlo_ref, bias_ref, tri_ref, eidx_ref, wts_ref, rank_ref, cnt_ref, seen):
    @pl.when(pl.program_id(0) == 0)
    def _():
        seen[...] = jnp.zeros_like(seen)

    a_hi, a_lo = _split_hi_lo(h_ref[...])
    z = _dot_nt(whi_ref[...], a_hi) + _dot_nt(wlo_ref[...], a_hi) + _dot_nt(whi_ref[...], a_lo)
    scores = jax.nn.sigmoid(z)
    biased = scores + bias_ref[...]
    n_e, tm = scores.shape
    gs = n_e // N_GROUP
    neg = -jnp.inf
    row = lax.broadcasted_iota(jnp.int32, (n_e, tm), 0).astype(F32)

    group_scores = []
    grp_row = lax.broadcasted_iota(jnp.int32, (gs, tm), 0).astype(F32)
    for g in range(N_GROUP):
        xg, rg = biased[g * gs:(g + 1) * gs], grp_row
        m1, i1 = _first_argmax(xg, rg, n_e)
        m2 = jnp.max(jnp.where(rg == i1, neg, xg), axis=0, keepdims=True)
        group_scores.append(m1 + m2)
    grp = jnp.concatenate(group_scores, axis=0)
    grow = lax.broadcasted_iota(jnp.int32, (N_GROUP, tm), 0).astype(F32)
    keep = jnp.zeros((N_GROUP, tm), F32)
    for _ in range(TOPK_GROUP):
        _, gi = _first_argmax(grp, grow, N_GROUP)
        hit = grow == gi
        keep = jnp.where(hit, 1.0, keep)
        grp = jnp.where(hit, neg, grp)
    emask = jnp.concatenate([jnp.broadcast_to(keep[g:g + 1], (gs, tm)) for g in range(N_GROUP)], axis=0)
    cand = jnp.where(emask > 0.0, biased, neg)

    picked = jnp.zeros((n_e, tm), F32)
    eidx, wts = [], []
    for _ in range(TOP_K):
        _, ei = _first_argmax(cand, row, n_e)
        hit = row == ei
        eidx.append(ei)
        wts.append(jnp.sum(jnp.where(hit, scores, 0.0), axis=0, keepdims=True))
        picked = jnp.where(hit, 1.0, picked)
        cand = jnp.where(hit, neg, cand)
    total = wts[0]
    for w in wts[1:]:
        total = total + w
    before = _dot(picked.astype(BF16), tri_ref[...]) + seen[...]
    ranks = [jnp.sum(jnp.where(row == ei, before, 0.0), axis=0, keepdims=True) for ei in eidx]
    seen[...] = seen[...] + jnp.sum(picked, axis=1, keepdims=True)
    eidx_ref[...] = jnp.concatenate(eidx, axis=0).astype(jnp.int32)
    wts_ref[...] = jnp.concatenate([w / total * ROUTED_SCALE for w in wts], axis=0)
    rank_ref[...] = jnp.concatenate(ranks, axis=0).astype(jnp.int32)
    cnt_ref[...] = seen[...].astype(jnp.int32)


def _router(h, router_w, router_bias, tm):
    t, d = h.shape
    n_e = router_w.shape[1]
    w_hi, w_lo = _split_hi_lo(router_w.T)
    i = np.arange(tm)
    tri = jnp.asarray((i[:, None] < i[None, :]).astype(np.float32), BF16)
    tok_spec = pl.BlockSpec((TOP_K, tm), lambda i: (0, i))
    return pl.pallas_call(
        _router_kernel,
        grid=(t // tm,),
        in_specs=[pl.BlockSpec((tm, d), lambda i: (i, 0)),
                  pl.BlockSpec((n_e, d), lambda i: (0, 0)),
                  pl.BlockSpec((n_e, d), lambda i: (0, 0)),
                  pl.BlockSpec((n_e, 1), lambda i: (0, 0)),
                  pl.BlockSpec((tm, tm), lambda i: (0, 0))],
        out_specs=[tok_spec, tok_spec, tok_spec, pl.BlockSpec((n_e, 1), lambda i: (0, 0))],
        out_shape=[jax.ShapeDtypeStruct((TOP_K, t), jnp.int32), jax.ShapeDtypeStruct((TOP_K, t), F32),
                   jax.ShapeDtypeStruct((TOP_K, t), jnp.int32), jax.ShapeDtypeStruct((n_e, 1), jnp.int32)],
        scratch_shapes=[pltpu.VMEM((n_e, 1), F32)],
        compiler_params=_cparams("arbitrary"),
        name="router",
    )(h, w_hi, w_lo, router_bias.reshape(n_e, 1).astype(F32), tri)


def _slot_kernel(eidx_ref, rank_ref, seg_ref, slot_ref):
    n_e = seg_ref.shape[0]
    tm = eidx_ref.shape[1]
    row = lax.broadcasted_iota(jnp.int32, (n_e, tm), 0)
    seg = seg_ref[...]
    slots = []
    for k in range(eidx_ref.shape[0]):
        start = jnp.sum(jnp.where(row == eidx_ref[k:k + 1, :], seg, 0.0), axis=0, keepdims=True)
        slots.append(start.astype(jnp.int32) + rank_ref[k:k + 1, :])
    slot_ref[...] = jnp.concatenate(slots, axis=0)


def _assignment_slots(eidx_t, rank_t, seg_start, tm):
    k, t = eidx_t.shape
    n_e = seg_start.shape[0]
    tok_spec = pl.BlockSpec((k, tm), lambda i: (0, i))
    return pl.pallas_call(
        _slot_kernel,
        grid=(t // tm,),
        in_specs=[tok_spec, tok_spec, pl.BlockSpec((n_e, 1), lambda i: (0, 0))],
        out_specs=tok_spec,
        out_shape=jax.ShapeDtypeStruct((k, t), jnp.int32),
        compiler_params=_cparams("arbitrary"),
        name="assignment_slots",
    )(eidx_t, rank_t, seg_start.reshape(n_e, 1).astype(F32))


def _expert_layout(counts, n_assign, unit_rows):
    n_e = counts.shape[0]
    padded = (counts + ROW_ALIGN - 1) // ROW_ALIGN * ROW_ALIGN
    seg_end = jnp.cumsum(padded)
    seg_start = seg_end - padded
    n_slots = n_assign + n_e * ROW_ALIGN
    units_per_e = (counts + unit_rows - 1) // unit_rows
    unit_end = jnp.cumsum(units_per_e)
    unit_first = unit_end - units_per_e
    n_units = unit_end[-1]
    max_units = n_e + n_assign // unit_rows
    uid = jnp.arange(max_units, dtype=jnp.int32)
    ue = jnp.minimum(jnp.sum(uid[:, None] >= unit_end[None, :], axis=1), n_e - 1).astype(jnp.int32)
    j = uid - unit_first[ue]
    valid = uid < n_units
    last_e = ue[jnp.maximum(n_units - 1, 0)]
    unit_e = jnp.where(valid, ue, last_e)
    unit_start = jnp.where(valid, seg_start[ue] + j * unit_rows, 0)
    unit_nrows = jnp.where(valid, jnp.clip(padded[ue] - j * unit_rows, 0, unit_rows), 0)
    n_units_tail = jnp.stack([n_units, seg_end[-1]]).astype(jnp.int32)
    pad_group = jnp.where(padded > 0, seg_end - ROW_ALIGN, -1).astype(jnp.int32)
    return (seg_start.astype(jnp.int32), pad_group, unit_e.astype(jnp.int32), unit_start.astype(jnp.int32),
            unit_nrows.astype(jnp.int32), n_units_tail, n_slots, max_units)


def _dispatch_kernel(slot_ref, pad_ref, tail_ref, x_ref, xs_hbm, zrows, zsem, sem, *, tm, k, t):
    i = pl.program_id(0)

    @pl.when(i == 0)
    def _():
        zrows[...] = jnp.zeros_like(zrows)
        tail0 = pl.multiple_of(tail_ref[1], ROW_ALIGN)
        tail_groups = (xs_hbm.shape[0] - tail0) // ROW_ALIGN
        n_e = pad_ref.shape[0]

        def zero_group(r0):
            return pltpu.make_async_copy(zrows, xs_hbm.at[pl.ds(pl.multiple_of(r0, ROW_ALIGN), ROW_ALIGN)], zsem)

        def pad_start(e, carry):
            @pl.when(pad_ref[e] >= 0)
            def _():
                zero_group(pad_ref[e]).start()
            return carry

        def pad_wait(e, carry):
            @pl.when(pad_ref[e] >= 0)
            def _():
                zero_group(pad_ref[e]).wait()
            return carry

        def tail_start(g, carry):
            zero_group(tail0 + g * ROW_ALIGN).start()
            return carry

        def tail_wait(g, carry):
            zero_group(tail0 + g * ROW_ALIGN).wait()
            return carry

        lax.fori_loop(0, n_e, pad_start, 0)
        lax.fori_loop(0, tail_groups, tail_start, 0)
        lax.fori_loop(0, n_e, pad_wait, 0)
        lax.fori_loop(0, tail_groups, tail_wait, 0)

    def issue(r, carry):
        for kk in range(k):
            s = slot_ref[kk * t + i * tm + r]
            pltpu.make_async_copy(x_ref.at[pl.ds(r, 1)], xs_hbm.at[pl.ds(s, 1)], sem).start()
        return carry

    lax.fori_loop(0, tm, issue, 0)
    for kk in range(k):
        pltpu.make_async_copy(x_ref, xs_hbm.at[pl.ds(0, tm)], sem).wait()


def _dispatch(xp, slot_flat, pad_group, n_units_tail, n_slots, tm, k):
    t, dh = xp.shape
    grid_spec = pltpu.PrefetchScalarGridSpec(
        num_scalar_prefetch=3,
        grid=(t // tm,),
        in_specs=[pl.BlockSpec((tm, dh), lambda i, *_: (i, 0))],
        out_specs=pl.BlockSpec(memory_space=pl.ANY),
        scratch_shapes=[pltpu.VMEM((ROW_ALIGN, dh), xp.dtype),
                        pltpu.SemaphoreType.DMA(()), pltpu.SemaphoreType.DMA(())])
    return pl.pallas_call(
        functools.partial(_dispatch_kernel, tm=tm, k=k, t=t),
        grid_spec=grid_spec,
        out_shape=jax.ShapeDtypeStruct((n_slots, dh), xp.dtype),
        compiler_params=_cparams("arbitrary"),
        name="moe_dispatch",
    )(slot_flat, pad_group, n_units_tail, xp)


def _moe_kernel(ue_ref, us_ref, ur_ref, nu_ref,
                xs_hbm, wg_ref, wu_ref, wd_ref, y_hbm,
                xbuf, xbf, hbuf, ybuf, zrows, wgb, wub, wdb, xsem, ysem, zsem, *, nf, nd):
    u = pl.program_id(0)
    s = pl.program_id(1)
    n_units = nu_ref[0]
    tn = ybuf.shape[2]

    def for_each(count, fn):
        def body(j, carry):
            fn(j)
            return carry
        lax.fori_loop(0, count, body, 0)

    def x_group(unit, g):
        r0 = pl.multiple_of(g * ROW_ALIGN, ROW_ALIGN)
        src0 = pl.multiple_of(us_ref[unit], ROW_ALIGN) + r0
        return pltpu.make_async_copy(xs_hbm.at[pl.ds(src0, ROW_ALIGN)],
                                     xbuf.at[unit % 2, pl.ds(r0, ROW_ALIGN)], xsem.at[unit % 2])

    def y_slot(unit, n):
        return (unit * nd + n) % 2

    def y_group(unit, n, g):
        r0 = pl.multiple_of(g * ROW_ALIGN, ROW_ALIGN)
        dst0 = pl.multiple_of(us_ref[unit], ROW_ALIGN) + r0
        return pltpu.make_async_copy(ybuf.at[y_slot(unit, n), pl.ds(r0, ROW_ALIGN)],
                                     y_hbm.at[pl.ds(dst0, ROW_ALIGN), pl.ds(pl.multiple_of(n * tn, tn), tn)],
                                     ysem.at[y_slot(unit, n)])

    def groups(unit):
        return ur_ref[unit] // ROW_ALIGN

    @pl.when(jnp.logical_and(u == 0, s == 0))
    def _():
        xbuf[...] = jnp.zeros_like(xbuf)
        zrows[...] = jnp.zeros_like(zrows)
        tail0 = pl.multiple_of(nu_ref[1], ROW_ALIGN)

        def tail_copy(g):
            r0 = pl.multiple_of(tail0 + g * ROW_ALIGN, ROW_ALIGN)
            return pltpu.make_async_copy(zrows, y_hbm.at[pl.ds(r0, ROW_ALIGN)], zsem)

        tail_groups = (y_hbm.shape[0] - tail0) // ROW_ALIGN
        for_each(tail_groups, lambda g: tail_copy(g).start())
        for_each(tail_groups, lambda g: tail_copy(g).wait())

    @pl.when(u < n_units)
    def _():
        def for_row_count(fn):
            fn(xbf.shape[0])

        @pl.when(s == 0)
        def _():
            @pl.when(u == 0)
            def _():
                for_each(groups(u), lambda g: x_group(u, g).start())

            for_each(groups(u), lambda g: x_group(u, g).wait())

            @pl.when(u + 1 < n_units)
            def _():
                for_each(groups(u + 1), lambda g: x_group(u + 1, g).start())

            def unpack(m):
                xbf[0:m, :] = _unpack_bf16_pairs(xbuf[u % 2, 0:m, :])

            for_row_count(unpack)

        @pl.when(s < nf)
        def _():
            wgb[...] = wg_ref[...].astype(BF16)
            wub[...] = wu_ref[...].astype(BF16)

            def up_proj(m):
                xb = xbf[0:m, :]
                g = _dot(xb, wgb[...])
                up = _dot(xb, wub[...])
                hbuf[s, 0:m, :] = (_silu(g) * up).astype(BF16)

            for_row_count(up_proj)

        @pl.when(s >= nf)
        def _():
            n = s - nf
            wdb[...] = wd_ref[...].astype(BF16)

            def down_proj(m):
                h = jnp.concatenate([hbuf[j, 0:m, :] for j in range(nf)], axis=1)
                ybuf[y_slot(u, n), 0:m, :] = _dot(h, wdb[...])

            for_row_count(down_proj)
            for_each(groups(u), lambda g: y_group(u, n, g).start())

            @pl.when(n > 0)
            def _():
                for_each(groups(u), lambda g: y_group(u, n - 1, g).wait())

            @pl.when(jnp.logical_and(n == 0, u > 0))
            def _():
                for_each(groups(u - 1), lambda g: y_group(u - 1, nd - 1, g).wait())

            @pl.when(jnp.logical_and(n == nd - 1, u == n_units - 1))
            def _():
                for_each(groups(u), lambda g: y_group(u, n, g).wait())


def _grouped_swiglu(xs, unit_e, unit_start, unit_nrows, n_units_tail, max_units,
                    w_gate, w_up, w_down, unit_rows, name):
    n_slots, dh = xs.shape
    d = 2 * dh
    n_e, _, de = w_gate.shape
    tf = _tile(de, MOE_F_TILE)
    nf = de // tf
    tn = _tile(d, MOE_OUT_TILE)
    nd = d // tn

    def up_map(u, s, ue, us, ur, nu):
        return ue[u], 0, jnp.where(u < nu[0], jnp.minimum(s, nf - 1), nf - 1)

    def down_map(u, s, ue, us, ur, nu):
        prev = jnp.maximum(u - 1, 0)
        in_up = jnp.logical_and(s < nf, u > 0)
        e = jnp.where(in_up, ue[prev], ue[u])
        n = jnp.where(in_up, nd - 1, jnp.maximum(s - nf, 0))
        return e, 0, jnp.where(u < nu[0], n, nd - 1)

    grid_spec = pltpu.PrefetchScalarGridSpec(
        num_scalar_prefetch=4,
        grid=(max_units, nf + nd),
        in_specs=[pl.BlockSpec(memory_space=pl.ANY),
                  pl.BlockSpec((None, d, tf), up_map),
                  pl.BlockSpec((None, d, tf), up_map),
                  pl.BlockSpec((None, de, tn), down_map)],
        out_specs=pl.BlockSpec(memory_space=pl.ANY),
        scratch_shapes=[pltpu.VMEM((2, unit_rows, dh), jnp.uint32),
                        pltpu.VMEM((unit_rows, d), BF16),
                        pltpu.VMEM((nf, unit_rows, tf), BF16),
                        pltpu.VMEM((2, unit_rows, tn), F32),
                        pltpu.VMEM((ROW_ALIGN, d), F32),
                        pltpu.VMEM((d, tf), BF16), pltpu.VMEM((d, tf), BF16), pltpu.VMEM((de, tn), BF16),
                        pltpu.SemaphoreType.DMA((2,)), pltpu.SemaphoreType.DMA((2,)),
                        pltpu.SemaphoreType.DMA(())])
    return pl.pallas_call(
        functools.partial(_moe_kernel, nf=nf, nd=nd),
        grid_spec=grid_spec,
        out_shape=jax.ShapeDtypeStruct((n_slots, d), F32),
        compiler_params=_cparams("arbitrary", "arbitrary"),
        name=name,
    )(unit_e, unit_start, unit_nrows, n_units_tail, xs, w_gate, w_up, w_down)


def _combine_kernel(pos_ref, ys_hbm, wts_ref, ysh_ref, x1_ref, mod_ref, fw_ref, out_ref, buf, sem, *,
                    tm, k, t, row0):
    i = pl.program_id(0)
    n = pl.num_programs(0)

    def gather(step):
        def issue(r, carry):
            for kk in range(k):
                p = pos_ref[kk * t + row0 + step * tm + r]
                pltpu.make_async_copy(ys_hbm.at[pl.ds(p, 1)], buf.at[step % 2, kk, pl.ds(r, 1)],
                                      sem.at[step % 2]).start()
            return carry

        lax.fori_loop(0, tm, issue, 0)

    @pl.when(i == 0)
    def _():
        gather(i)

    @pl.when(i + 1 < n)
    def _():
        gather(i + 1)

    for kk in range(k):
        pltpu.make_async_copy(ys_hbm.at[pl.ds(0, tm)], buf.at[i % 2, kk], sem.at[i % 2]).wait()
    acc = buf[i % 2, 0] * wts_ref[:, 0:1]
    for kk in range(1, k):
        acc = acc + buf[i % 2, kk] * wts_ref[:, kk:kk + 1]
    ff = acc + ysh_ref[...]
    x2 = x1_ref[...] + mod_ref[0, 5:6, :] * ff
    out_ref[...] = x2 * lax.rsqrt(jnp.mean(x2 * x2, axis=-1, keepdims=True) + EPS) * fw_ref[...]


def _combine(y_sorted, assign_slot, wts, y_shared, x1, mod3, final_w, tm, ctx_rows, lat_seq, first_row, n_rows):
    t, d = x1.shape
    k = wts.shape[1]
    cb = ctx_rows // tm
    bps = lat_seq // tm
    ctx_row = (t - ctx_rows) // lat_seq
    b0 = first_row // tm
    mod_map = _mod_row_map(cb, bps, ctx_row)
    grid_spec = pltpu.PrefetchScalarGridSpec(
        num_scalar_prefetch=1,
        grid=(n_rows // tm,),
        in_specs=[pl.BlockSpec(memory_space=pl.ANY),
                  pl.BlockSpec((tm, k), lambda i, pos: (i + b0, 0)),
                  pl.BlockSpec((tm, d), lambda i, pos: (i + b0, 0)),
                  pl.BlockSpec((tm, d), lambda i, pos: (i + b0, 0)),
                  pl.BlockSpec((1, 6, d), lambda i, pos: mod_map(i + b0)),
                  pl.BlockSpec((1, d), lambda i, pos: (0, 0))],
        out_specs=pl.BlockSpec((tm, d), lambda i, pos: (i, 0)),
        scratch_shapes=[pltpu.VMEM((2, k, tm, d), F32), pltpu.SemaphoreType.DMA((2,))])
    return pl.pallas_call(
        functools.partial(_combine_kernel, tm=tm, k=k, t=t, row0=first_row),
        grid_spec=grid_spec,
        out_shape=jax.ShapeDtypeStruct((n_rows, d), F32),
        compiler_params=_cparams("arbitrary"),
        name="moe_combine",
    )(assign_slot, y_sorted, wts, y_shared, x1, mod3, final_w)


def _grid_pos_embed(n_tokens, d_model):
    rows = n_tokens // GRID_W
    r, col = jnp.meshgrid(jnp.arange(rows, dtype=F32), jnp.arange(GRID_W, dtype=F32), indexing="ij")
    n_freq = d_model // 4
    omega = 1.0 / (10000.0 ** (jnp.arange(n_freq, dtype=F32) / n_freq))

    def enc(p):
        ang = p.reshape(-1)[:, None] * omega[None, :]
        return jnp.concatenate([jnp.sin(ang), jnp.cos(ang)], axis=-1)

    return jnp.concatenate([enc(r), enc(col)], axis=-1)


def _pad_rows(a, rows, row0=0):
    return jnp.zeros((rows,) + a.shape[1:], a.dtype).at[row0:row0 + a.shape[0]].set(a)


def kernel(x_prompt, x_sample, state_hgrn_fwd, state_hgrn_bwd, state_gla_fwd, state_gla_bwd, c, c_ctx, norm1_w, norm2_w, ada_w, ada_b, w_in, hg_lb_logits, gla_gk_up_fwd, gla_gk_bias_fwd, gla_gk_up_bwd, gla_gk_bias_bwd, hg_norm_w, gla_norm_w, w_out, router_w, router_bias, exp_w_gate, exp_w_up, exp_w_down, sh_w_gate, sh_w_up, sh_w_down, final_norm_w):
    n_ctx, ctx_seq, d = x_prompt.shape
    n_lat, lat_seq, _ = x_sample.shape
    depth = norm1_w.shape[0]
    assert depth == 1, "one trunk layer"
    _, _, hg_heads, hg_dk, hg_dv = state_hgrn_fwd.shape
    _, _, gla_heads, gla_dk, gla_dv = state_gla_fwd.shape
    assert hg_dk == hg_dv
    rank = gla_gk_up_fwd.shape[1]
    n_experts = exp_w_gate.shape[1]
    hg_w = hg_heads * hg_dk
    gla_kw = gla_heads * gla_dk
    gla_w = gla_heads * gla_dv
    tp, ts = n_ctx * ctx_seq, n_lat * lat_seq
    t = tp + ts
    assert tp % lat_seq == 0 and 2 * rank <= LANES and n_lat < 8
    tm = min(256, ctx_seq)
    layer = 0

    lb_all = jnp.cumsum(jax.nn.softmax(hg_lb_logits.astype(F32), axis=1), axis=1)
    pos = _grid_pos_embed(lat_seq, d)

    c_rows = jnp.zeros((8, d), F32).at[:n_lat].set(c).at[n_lat].set(c_ctx)
    mod = _ada(c_rows, ada_w[layer], ada_b[layer].reshape(1, -1))
    mod3 = mod.reshape(8, 6, d)

    x0, h1 = _norm1(x_prompt.reshape(tp, d), x_sample.reshape(ts, d), pos, mod3,
                    norm1_w[layer].reshape(1, d), tm, lat_seq)

    lr0 = 5 * hg_w + 2 * gla_kw + gla_w
    w_l = w_in[layer]
    w_main = jnp.concatenate([w_l[:, :lr0], w_l[:, lr0 + 2 * rank:]], axis=1).astype(BF16)
    w_lr = jnp.zeros((d, LANES), BF16).at[:, :2 * rank].set(w_l[:, lr0:lr0 + 2 * rank].astype(BF16))
    proj = _matmul(h1, w_main, 1024, 1024, "in_proj")
    lr = _matmul(h1, w_lr, 1024, LANES, "in_proj_lr")

    consts = {c: _level_constants(c) for c in {min(HGRN_CTX_CHUNK, ctx_seq), min(GLA_CTX_CHUNK, ctx_seq),
                                               min(HGRN_LAT_CHUNK, lat_seq), min(GLA_LAT_CHUNK, lat_seq)}}
    up_f = _pad_rows(gla_gk_up_fwd[layer], LANES, 0)
    up_b = _pad_rows(gla_gk_up_bwd[layer], LANES, rank)
    bias_f = gla_gk_bias_fwd[layer].reshape(1, gla_kw)
    bias_b = gla_gk_bias_bwd[layer].reshape(1, gla_kw)
    hg_args = dict(heads=hg_heads, dk=hg_dk)
    gla_args = dict(heads=gla_heads, dk=gla_dk, dv=gla_dv, col0=5 * hg_w)

    hg_ctx, new_hf, new_hb = _hgrn(proj, lb_all[0, layer], lb_all[1, layer], hg_norm_w[layer], None, None,
                                   consts[min(HGRN_CTX_CHUNK, ctx_seq)], batch=n_ctx, seq=ctx_seq, row_block0=0, **hg_args)
    hg_lat, _, _ = _hgrn(proj, lb_all[0, layer], lb_all[1, layer], hg_norm_w[layer],
                         state_hgrn_fwd[:, layer], state_hgrn_bwd[:, layer], consts[min(HGRN_LAT_CHUNK, lat_seq)],
                         batch=n_lat, seq=lat_seq, row_block0=tp // lat_seq, **hg_args)
    gla_ctx, new_gf, new_gb = _gla(proj, lr, up_f, up_b, bias_f, bias_b, gla_norm_w[layer], None, None,
                                   consts[min(GLA_CTX_CHUNK, ctx_seq)], batch=n_ctx, seq=ctx_seq, row_block0=0, **gla_args)
    gla_lat, _, _ = _gla(proj, lr, up_f, up_b, bias_f, bias_b, gla_norm_w[layer],
                         state_gla_fwd[:, layer], state_gla_bwd[:, layer], consts[min(GLA_LAT_CHUNK, lat_seq)],
                         batch=n_lat, seq=lat_seq, row_block0=tp // lat_seq, **gla_args)

    w_o = w_out[layer].astype(BF16)
    mix_ctx = _matmul2(hg_ctx, w_o[:hg_w], gla_ctx, w_o[hg_w:], 1024, 1024, "out_proj_ctx")
    mix_lat = _matmul2(hg_lat, w_o[:hg_w], gla_lat, w_o[hg_w:], 1024, 1024, "out_proj_lat")

    x1, h2, h2p = _norm2(x0, mix_ctx, mix_lat, mod3, norm2_w[layer].reshape(1, d), tm, lat_seq)

    tok_tile = _tile(t, 512)
    eidx_t, wts_t, rank_t, counts = _router(h2, router_w[layer], router_bias[layer], tok_tile)
    seg_start, pad_group, unit_e, unit_start, unit_nrows, n_units_tail, n_slots, max_units = _expert_layout(
        counts.reshape(n_experts), t * TOP_K, MOE_UNIT_ROWS)
    slot_flat = _assignment_slots(eidx_t, rank_t, seg_start, tok_tile).reshape(TOP_K * t)
    xs = _dispatch(h2p, slot_flat, pad_group, n_units_tail, n_slots, tm, TOP_K)
    y_sorted = _grouped_swiglu(xs, unit_e, unit_start, unit_nrows, n_units_tail, max_units,
                               exp_w_gate[layer], exp_w_up[layer], exp_w_down[layer], MOE_UNIT_ROWS, "moe_experts")

    assert t % MOE_UNIT_ROWS == 0
    sh_units = t // MOE_UNIT_ROWS
    sh_uid = jnp.arange(sh_units, dtype=jnp.int32)
    y_shared = _grouped_swiglu(h2p, jnp.zeros((sh_units,), jnp.int32), sh_uid * MOE_UNIT_ROWS,
                               jnp.full((sh_units,), MOE_UNIT_ROWS, jnp.int32),
                               jnp.array([sh_units, t], jnp.int32), sh_units,
                               sh_w_gate[layer][None], sh_w_up[layer][None], sh_w_down[layer][None],
                               MOE_UNIT_ROWS, "shared_expert")

    fin = (y_sorted, slot_flat, wts_t.T, y_shared, x1, mod3, final_norm_w.reshape(1, d), 64, tp, lat_seq)
    y_prompt = _combine(*fin, 0, tp).reshape(n_ctx, ctx_seq, d)
    y_sample = _combine(*fin, tp, ts).reshape(n_lat, lat_seq, d)
    return (y_prompt, y_sample, new_hf[:, None], new_hb[:, None], new_gf[:, None], new_gb[:, None])
```

```python
import functools
import math

import numpy as np
import jax
import jax.numpy as jnp
from jax import lax
from jax.experimental import pallas as pl
from jax.experimental.pallas import tpu as pltpu

F32 = jnp.float32
BF16 = jnp.bfloat16

GRID_W = 64
TOP_K = 8
N_GROUP = 8
TOPK_GROUP = 4
ROUTED_SCALE = 2.5
GLA_LOGIT_NORM = 16.0
EPS = 1e-6

V7X_VMEM_LIMIT_BYTES = 56 * 1024 * 1024
LANES = 128
ROW_ALIGN = 16
HGRN_HEADS_PER_STEP = 2
HGRN_CTX_CHUNK, HGRN_LAT_CHUNK, GLA_CTX_CHUNK, GLA_LAT_CHUNK = 128, 128, 128, 256
MOE_UNIT_ROWS = 512
MOE_F_TILE = 256
MOE_OUT_TILE = 1024


def _cparams(*sem):
    return pltpu.CompilerParams(dimension_semantics=sem, vmem_limit_bytes=V7X_VMEM_LIMIT_BYTES)


def _tile(n, preferred):
    t = min(preferred, n)
    while n % t:
        t //= 2
    return t


def _dot(a, b):
    return jnp.dot(a, b, preferred_element_type=F32)


def _dot_nt(a, b):
    return lax.dot_general(a, b, (((1,), (1,)), ((), ())), preferred_element_type=F32)


def _dot_tn(a, b):
    return lax.dot_general(a, b, (((0,), (0,)), ((), ())), preferred_element_type=F32)


def _silu(x):
    return x * jax.nn.sigmoid(x)


def _split_hi_lo(x):
    hi = x.astype(BF16)
    return hi, (x - hi.astype(F32)).astype(BF16)


def _ada_kernel(c_ref, w_ref, b_ref, o_ref):
    a = _silu(c_ref[...]).astype(BF16)
    o_ref[...] = _dot(a, w_ref[...].astype(BF16)) + b_ref[...]


def _ada(c_rows, ada_w, ada_b):
    d, n = ada_w.shape
    rows = c_rows.shape[0]
    tn = min(512, n)
    return pl.pallas_call(
        _ada_kernel,
        grid=(n // tn,),
        in_specs=[pl.BlockSpec((rows, d), lambda j: (0, 0)),
                  pl.BlockSpec((d, tn), lambda j: (0, j)),
                  pl.BlockSpec((1, tn), lambda j: (0, j))],
        out_specs=pl.BlockSpec((rows, tn), lambda j: (0, j)),
        out_shape=jax.ShapeDtypeStruct((rows, n), F32),
        compiler_params=_cparams("arbitrary"),
        name="ada_mod",
    )(c_rows, ada_w, ada_b)


def _modulated_norm(x, w, shift, scale):
    y = x * lax.rsqrt(jnp.mean(x * x, axis=-1, keepdims=True) + EPS) * w
    return y * (1.0 + scale) + shift


def _norm1_kernel(xp_ref, xs_ref, pos_ref, mod_ref, w_ref, x0_ref, h_ref, *, ctx_blocks):
    i = pl.program_id(0)

    def finish(x):
        x0_ref[...] = x
        h_ref[...] = _modulated_norm(x, w_ref[...], mod_ref[0, 0:1, :], mod_ref[0, 1:2, :]).astype(BF16)

    @pl.when(i < ctx_blocks)
    def _():
        finish(xp_ref[...])

    @pl.when(i >= ctx_blocks)
    def _():
        finish(xs_ref[...] + pos_ref[...])


def _mod_row_map(ctx_blocks, blocks_per_seq, ctx_row):
    return lambda i, *_: (jnp.where(i < ctx_blocks, ctx_row, (i - ctx_blocks) // blocks_per_seq), 0, 0)


def _norm1(xp, xs, pos, mod3, w, tm, lat_seq):
    tp, d = xp.shape
    ts = xs.shape[0]
    t = tp + ts
    cb = tp // tm
    bps = lat_seq // tm
    ctx_row = ts // lat_seq
    return pl.pallas_call(
        functools.partial(_norm1_kernel, ctx_blocks=cb),
        grid=(t // tm,),
        in_specs=[pl.BlockSpec((tm, d), lambda i: (jnp.minimum(i, cb - 1), 0)),
                  pl.BlockSpec((tm, d), lambda i: (jnp.maximum(i - cb, 0), 0)),
                  pl.BlockSpec((tm, d), lambda i: (jnp.maximum(i - cb, 0) % bps, 0)),
                  pl.BlockSpec((1, 6, d), _mod_row_map(cb, bps, ctx_row)),
                  pl.BlockSpec((1, d), lambda i: (0, 0))],
        out_specs=[pl.BlockSpec((tm, d), lambda i: (i, 0)),
                   pl.BlockSpec((tm, d), lambda i: (i, 0))],
        out_shape=[jax.ShapeDtypeStruct((t, d), F32), jax.ShapeDtypeStruct((t, d), BF16)],
        compiler_params=_cparams("arbitrary"),
        name="norm1",
    )(xp, xs, pos, mod3, w)


def _pack_bf16_pairs(h):
    half = h.shape[1] // 2
    bits = pltpu.bitcast(h.astype(BF16).astype(F32), jnp.uint32)
    return (bits[:, :half] >> 16) | (bits[:, half:] & jnp.uint32(0xFFFF0000))


def _unpack_bf16_pairs(w):
    lo = pltpu.bitcast(w << 16, F32).astype(BF16)
    hi = pltpu.bitcast(w & jnp.uint32(0xFFFF0000), F32).astype(BF16)
    return jnp.concatenate([lo, hi], axis=1)


def _norm2_kernel(x0_ref, mc_ref, ml_ref, mod_ref, w_ref, x1_ref, h_ref, hp_ref, *, ctx_blocks):
    i = pl.program_id(0)

    def finish(mix):
        x1 = x0_ref[...] + mod_ref[0, 2:3, :] * mix
        x1_ref[...] = x1
        h = _modulated_norm(x1, w_ref[...], mod_ref[0, 3:4, :], mod_ref[0, 4:5, :])
        h_ref[...] = h
        hp_ref[...] = _pack_bf16_pairs(h)

    @pl.when(i < ctx_blocks)
    def _():
        finish(mc_ref[...])

    @pl.when(i >= ctx_blocks)
    def _():
        finish(ml_ref[...])


def _norm2(x0, mix_ctx, mix_lat, mod3, w, tm, lat_seq):
    t, d = x0.shape
    tp = mix_ctx.shape[0]
    cb = tp // tm
    bps = lat_seq // tm
    ctx_row = (t - tp) // lat_seq
    return pl.pallas_call(
        functools.partial(_norm2_kernel, ctx_blocks=cb),
        grid=(t // tm,),
        in_specs=[pl.BlockSpec((tm, d), lambda i: (i, 0)),
                  pl.BlockSpec((tm, d), lambda i: (jnp.minimum(i, cb - 1), 0)),
                  pl.BlockSpec((tm, d), lambda i: (jnp.maximum(i - cb, 0), 0)),
                  pl.BlockSpec((1, 6, d), _mod_row_map(cb, bps, ctx_row)),
                  pl.BlockSpec((1, d), lambda i: (0, 0))],
        out_specs=[pl.BlockSpec((tm, d), lambda i: (i, 0)),
                   pl.BlockSpec((tm, d), lambda i: (i, 0)),
                   pl.BlockSpec((tm, d // 2), lambda i: (i, 0))],
        out_shape=[jax.ShapeDtypeStruct((t, d), F32), jax.ShapeDtypeStruct((t, d), F32),
                   jax.ShapeDtypeStruct((t, d // 2), jnp.uint32)],
        compiler_params=_cparams("arbitrary"),
        name="norm2",
    )(x0, mix_ctx, mix_lat, mod3, w)


def _mm_kernel(a_ref, b_ref, o_ref):
    o_ref[...] = _dot(a_ref[...], b_ref[...])


def _matmul(a, b, tm, tn, name):
    m, k = a.shape
    n = b.shape[1]
    tm, tn = _tile(m, tm), _tile(n, tn)
    return pl.pallas_call(
        _mm_kernel,
        grid=(n // tn, m // tm),
        in_specs=[pl.BlockSpec((tm, k), lambda j, i: (i, 0)),
                  pl.BlockSpec((k, tn), lambda j, i: (0, j))],
        out_specs=pl.BlockSpec((tm, tn), lambda j, i: (i, j)),
        out_shape=jax.ShapeDtypeStruct((m, n), F32),
        compiler_params=_cparams("arbitrary", "arbitrary"),
        name=name,
    )(a, b)


def _mm2_kernel(a1_ref, b1_ref, a2_ref, b2_ref, o_ref):
    o_ref[...] = _dot(a1_ref[...], b1_ref[...]) + _dot(a2_ref[...], b2_ref[...])


def _matmul2(a1, b1, a2, b2, tm, tn, name):
    m, k1 = a1.shape
    k2 = a2.shape[1]
    n = b1.shape[1]
    tm, tn = _tile(m, tm), _tile(n, tn)
    return pl.pallas_call(
        _mm2_kernel,
        grid=(n // tn, m // tm),
        in_specs=[pl.BlockSpec((tm, k1), lambda j, i: (i, 0)),
                  pl.BlockSpec((k1, tn), lambda j, i: (0, j)),
                  pl.BlockSpec((tm, k2), lambda j, i: (i, 0)),
                  pl.BlockSpec((k2, tn), lambda j, i: (0, j))],
        out_specs=pl.BlockSpec((tm, tn), lambda j, i: (i, j)),
        out_shape=jax.ShapeDtypeStruct((m, n), F32),
        compiler_params=_cparams("arbitrary", "arbitrary"),
        name=name,
    )(a1, b1, a2, b2)


def _level_constants(c):
    nlev = int(math.log2(c))
    i = np.arange(c)[:, None]
    t = np.arange(c)[None, :]
    groups = [t <= i]
    masks = [i == t]
    for lev in range(nlev):
        s = 1 << lev
        base = (i // (2 * s)) * (2 * s)
        pivot = base + s - 1
        right = i >= base + s
        if lev > 0:
            groups.append(np.where(right, (t > pivot) & (t <= i), (t > i) & (t <= pivot)))
        same = (i // (2 * s)) == (t // (2 * s))
        masks.append(same & right & ((t % (2 * s)) < s))
    masks.append(np.broadcast_to((i % 2) == 1, (c, c)))
    bmat_f = np.concatenate([g.astype(np.float32) for g in groups], axis=0)
    bmat_b = np.concatenate([g[::-1, ::-1].astype(np.float32) for g in groups], axis=0)
    mask_f = np.stack([m.astype(np.float32) for m in masks])
    mask_b = np.stack([m[::-1, ::-1].astype(np.float32) for m in masks])
    return (jnp.asarray(bmat_f, BF16), jnp.asarray(bmat_b, BF16),
            jnp.asarray(mask_f, F32), jnp.asarray(mask_b, F32), nlev)


def _chunk_step(q_s, k, v_bf, nl, st_ref, bmat_ref, mask_ref, c, nlev, total_row):
    n1, n2 = _split_hi_lo(nl)
    bm = bmat_ref[...]
    dec = _dot(bm, n1) + _dot(bm, n2)
    cum = dec[0:c]
    total = cum[total_row:total_row + 1]
    st = st_ref[...]
    o = _dot_nt((q_s * jnp.exp(-cum)).astype(BF16), st.astype(BF16))
    att = mask_ref[0] * _dot_nt(q_s.astype(BF16), k.astype(BF16))
    for lev in range(nlev):
        if lev == 0:
            dist = jnp.where(mask_ref[1 + nlev][:, 0:1] > 0.0, nl, 0.0)
        else:
            dist = dec[lev * c:(lev + 1) * c]
        e = jnp.exp(-dist)
        att = att + mask_ref[1 + lev] * _dot_nt((q_s * e).astype(BF16), (k * e).astype(BF16))
    o = o + _dot(att.astype(BF16), v_bf)
    k_hat = (k * jnp.exp(cum - total)).astype(BF16)
    st_ref[...] = st * jnp.exp(-total) + _dot_tn(v_bf, k_hat)
    return o


def _head_norm_out(of_scr, ob_scr, g_ref, nw_ref, out_ref):
    o = of_scr[...] + ob_scr[...]
    y = o * lax.rsqrt(jnp.mean(o * o, axis=-1, keepdims=True) + EPS) * nw_ref[...]
    out_ref[...] = (y * _silu(g_ref[...])).astype(BF16)


def _hgrn_kernel(*refs, c, nchunks, nlev, scale, has_s0):
    (q_ref, ff_ref, fb_ref, i_ref, g_ref, lbf_ref, lbb_ref, nw_ref,
     bmf_ref, bmb_ref, mkf_ref, mkb_ref) = refs[:12]
    refs = refs[12:]
    if has_s0:
        s0f_ref, s0b_ref = refs[:2]
        refs = refs[2:]
    out_ref, sf_ref, sb_ref, of_scr, ob_scr, stf_scr, stb_scr = refs
    hp, dk = stf_scr.shape[0], stf_scr.shape[1]

    for j in range(hp):
        if has_s0:
            stf_scr[j] = s0f_ref[j].T
            stb_scr[j] = s0b_ref[j].T
        else:
            stf_scr[j] = jnp.zeros((dk, dk), F32)
            stb_scr[j] = jnp.zeros((dk, dk), F32)

    def one_direction(j, chunk, gate_ref, lb_ref, st_scr, bm_ref, mk_ref, o_scr, total_row):
        rows = pl.ds(pl.multiple_of(chunk * c, c), c)
        cols = slice(j * dk, (j + 1) * dk)
        lb = lb_ref[:, cols]
        f = lb + (1.0 - lb) * jax.nn.sigmoid(gate_ref[rows, cols])
        o_scr[rows, cols] = _chunk_step(q_ref[rows, cols] * scale, 1.0 - f, i_ref[rows, cols].astype(BF16),
                                        -jnp.log(f), st_scr.at[j], bm_ref, mk_ref, c, nlev, total_row)

    def body(ci, carry):
        for j in range(hp):
            one_direction(j, ci, ff_ref, lbf_ref, stf_scr, bmf_ref, mkf_ref, of_scr, c - 1)
            one_direction(j, nchunks - 1 - ci, fb_ref, lbb_ref, stb_scr, bmb_ref, mkb_ref, ob_scr, 0)
        return carry

    lax.fori_loop(0, nchunks, body, 0)
    for j in range(hp):
        cols = slice(j * dk, (j + 1) * dk)
        o = of_scr[:, cols] + ob_scr[:, cols]
        y = o * lax.rsqrt(jnp.mean(o * o, axis=-1, keepdims=True) + EPS) * nw_ref[...]
        out_ref[:, cols] = (y * _silu(g_ref[:, cols])).astype(BF16)
        sf_ref[j] = stf_scr[j].T
        sb_ref[j] = stb_scr[j].T


def _hgrn(proj, lb_f, lb_b, norm_w, s0_f, s0_b, consts, *, batch, seq, row_block0, heads, dk):
    bmf, bmb, mkf, mkb, nlev = consts
    c = mkf.shape[1]
    has_s0 = s0_f is not None
    hp = HGRN_HEADS_PER_STEP if heads % HGRN_HEADS_PER_STEP == 0 else 1
    steps = heads // hp
    w = hp * dk

    def col(group):
        return pl.BlockSpec((seq, w), lambda b, h: (row_block0 + b, group * steps + h))

    def const(arr):
        return pl.BlockSpec(arr.shape, lambda b, h: (0,) * arr.ndim)

    state_spec = pl.BlockSpec((None, hp, dk, dk), lambda b, h: (b, h, 0, 0))
    per_head = pl.BlockSpec((1, w), lambda b, h: (0, h))
    in_specs = [col(0), col(1), col(2), col(3), col(4), per_head, per_head,
                pl.BlockSpec((1, dk), lambda b, h: (0, 0)),
                const(bmf), const(bmb), const(mkf), const(mkb)]
    args = [proj, proj, proj, proj, proj, lb_f.reshape(1, heads * dk), lb_b.reshape(1, heads * dk),
            norm_w.reshape(1, dk), bmf, bmb, mkf, mkb]
    if has_s0:
        in_specs += [state_spec, state_spec]
        args += [s0_f, s0_b]
    return pl.pallas_call(
        functools.partial(_hgrn_kernel, c=c, nchunks=seq // c, nlev=nlev, scale=dk ** -0.5, has_s0=has_s0),
        grid=(batch, steps),
        in_specs=in_specs,
        out_specs=[pl.BlockSpec((seq, w), lambda b, h: (b, h)), state_spec, state_spec],
        out_shape=[jax.ShapeDtypeStruct((batch * seq, heads * dk), BF16),
                   jax.ShapeDtypeStruct((batch, heads, dk, dk), F32),
                   jax.ShapeDtypeStruct((batch, heads, dk, dk), F32)],
        scratch_shapes=[pltpu.VMEM((seq, w), F32), pltpu.VMEM((seq, w), F32),
                        pltpu.VMEM((hp, dk, dk), F32), pltpu.VMEM((hp, dk, dk), F32)],
        compiler_params=_cparams("arbitrary", "arbitrary"),
        name="hgrn_scan",
    )(*args)


def _gla_kernel(*refs, c, nchunks, nlev, scale, has_s0):
    (q_ref, k_ref, v_ref, g_ref, lr_ref, ufh_ref, ufl_ref, ubh_ref, ubl_ref, bf_ref, bb_ref, nw_ref,
     bmf_ref, bmb_ref, mkf_ref, mkb_ref) = refs[:16]
    refs = refs[16:]
    if has_s0:
        s0f_ref, s0b_ref = refs[:2]
        refs = refs[2:]
    out_ref, sf_ref, sb_ref, of_scr, ob_scr, stf_scr, stb_scr = refs

    if has_s0:
        stf_scr[...] = s0f_ref[...].T
        stb_scr[...] = s0b_ref[...].T
    else:
        stf_scr[...] = jnp.zeros_like(stf_scr)
        stb_scr[...] = jnp.zeros_like(stb_scr)

    def one_direction(chunk, uh_ref, ul_ref, bias_ref, st_scr, bm_ref, mk_ref, o_scr, total_row):
        rows = pl.ds(pl.multiple_of(chunk * c, c), c)
        lr_hi, lr_lo = _split_hi_lo(lr_ref[rows, :])
        z = _dot(lr_hi, uh_ref[...]) + _dot(lr_hi, ul_ref[...]) + _dot(lr_lo, uh_ref[...]) + bias_ref[...]
        nl = (jnp.maximum(-z, 0.0) + jnp.log(1.0 + jnp.exp(-jnp.abs(z)))) * (1.0 / GLA_LOGIT_NORM)
        o_scr[rows, :] = _chunk_step(q_ref[rows, :] * scale, k_ref[rows, :], v_ref[rows, :].astype(BF16),
                                     nl, st_scr, bm_ref, mk_ref, c, nlev, total_row)

    def body(ci, carry):
        one_direction(ci, ufh_ref, ufl_ref, bf_ref, stf_scr, bmf_ref, mkf_ref, of_scr, c - 1)
        one_direction(nchunks - 1 - ci, ubh_ref, ubl_ref, bb_ref, stb_scr, bmb_ref, mkb_ref, ob_scr, 0)
        return carry

    lax.fori_loop(0, nchunks, body, 0)
    _head_norm_out(of_scr, ob_scr, g_ref, nw_ref, out_ref)
    sf_ref[...] = stf_scr[...].T
    sb_ref[...] = stb_scr[...].T


def _gla(proj, lr, up_f, up_b, bias_f, bias_b, norm_w, s0_f, s0_b, consts, *,
         batch, seq, row_block0, heads, dk, dv, col0):
    bmf, bmb, mkf, mkb, nlev = consts
    c = mkf.shape[1]
    has_s0 = s0_f is not None
    kw = heads * dk
    q0, k0, v0, g0 = col0 // dk, (col0 + kw) // dk, (col0 + 2 * kw) // dv, (col0 + 2 * kw + heads * dv) // dv

    def const(arr):
        return pl.BlockSpec(arr.shape, lambda b, h: (0,) * arr.ndim)

    def head_cols(arr):
        return pl.BlockSpec((arr.shape[0], dk), lambda b, h: (0, h))

    state_spec = pl.BlockSpec((None, None, dk, dv), lambda b, h: (b, h, 0, 0))
    ufh, ufl = _split_hi_lo(up_f)
    ubh, ubl = _split_hi_lo(up_b)
    in_specs = [pl.BlockSpec((seq, dk), lambda b, h: (row_block0 + b, q0 + h)),
                pl.BlockSpec((seq, dk), lambda b, h: (row_block0 + b, k0 + h)),
                pl.BlockSpec((seq, dv), lambda b, h: (row_block0 + b, v0 + h)),
                pl.BlockSpec((seq, dv), lambda b, h: (row_block0 + b, g0 + h)),
                pl.BlockSpec((seq, LANES), lambda b, h: (row_block0 + b, 0)),
                head_cols(ufh), head_cols(ufl), head_cols(ubh), head_cols(ubl),
                head_cols(bias_f), head_cols(bias_b),
                pl.BlockSpec((1, dv), lambda b, h: (0, 0)),
                const(bmf), const(bmb), const(mkf), const(mkb)]
    args = [proj, proj, proj, proj, lr, ufh, ufl, ubh, ubl, bias_f, bias_b, norm_w.reshape(1, dv),
            bmf, bmb, mkf, mkb]
    if has_s0:
        in_specs += [state_spec, state_spec]
        args += [s0_f, s0_b]
    return pl.pallas_call(
        functools.partial(_gla_kernel, c=c, nchunks=seq // c, nlev=nlev, scale=dk ** -0.5, has_s0=has_s0),
        grid=(batch, heads),
        in_specs=in_specs,
        out_specs=[pl.BlockSpec((seq, dv), lambda b, h: (b, h)), state_spec, state_spec],
        out_shape=[jax.ShapeDtypeStruct((batch * seq, heads * dv), BF16),
                   jax.ShapeDtypeStruct((batch, heads, dk, dv), F32),
                   jax.ShapeDtypeStruct((batch, heads, dk, dv), F32)],
        scratch_shapes=[pltpu.VMEM((seq, dv), F32), pltpu.VMEM((seq, dv), F32),
                        pltpu.VMEM((dv, dk), F32), pltpu.VMEM((dv, dk), F32)],
        compiler_params=_cparams("arbitrary", "arbitrary"),
        name="gla_scan",
    )(*args)


def _first_argmax(x, row, n):
    m = jnp.max(x, axis=0, keepdims=True)
    idx = jnp.min(jnp.where(x == m, row, float(n)), axis=0, keepdims=True)
    return m, idx


def _router_kernel(h_ref, whi_ref, wlo_ref, bias_ref, tri_ref, eidx_ref, wts_ref, rank_ref, cnt_ref, seen):
    @pl.when(pl.program_id(0) == 0)
    def _():
        seen[...] = jnp.zeros_like(seen)

    a_hi, a_lo = _split_hi_lo(h_ref[...])
    z = _dot_nt(whi_ref[...], a_hi) + _dot_nt(wlo_ref[...], a_hi) + _dot_nt(whi_ref[...], a_lo)
    scores = jax.nn.sigmoid(z)
    biased = scores + bias_ref[...]
    n_e, tm = scores.shape
    gs = n_e // N_GROUP
    neg = -jnp.inf
    row = lax.broadcasted_iota(jnp.int32, (n_e, tm), 0).astype(F32)

    group_scores = []
    grp_row = lax.broadcasted_iota(jnp.int32, (gs, tm), 0).astype(F32)
    for g in range(N_GROUP):
        xg, rg = biased[g * gs:(g + 1) * gs], grp_row
        m1, i1 = _first_argmax(xg, rg, n_e)
        m2 = jnp.max(jnp.where(rg == i1, neg, xg), axis=0, keepdims=True)
        group_scores.append(m1 + m2)
    grp = jnp.concatenate(group_scores, axis=0)
    grow = lax.broadcasted_iota(jnp.int32, (N_GROUP, tm), 0).astype(F32)
    keep = jnp.zeros((N_GROUP, tm), F32)
    for _ in range(TOPK_GROUP):
        _, gi = _first_argmax(grp, grow, N_GROUP)
        hit = grow == gi
        keep = jnp.where(hit, 1.0, keep)
        grp = jnp.where(hit, neg, grp)
    emask = jnp.concatenate([jnp.broadcast_to(keep[g:g + 1], (gs, tm)) for g in range(N_GROUP)], axis=0)
    cand = jnp.where(emask > 0.0, biased, neg)

    picked = jnp.zeros((n_e, tm), F32)
    eidx, wts = [], []
    for _ in range(TOP_K):
        _, ei = _first_argmax(cand, row, n_e)
        hit = row == ei
        eidx.append(ei)
        wts.append(jnp.sum(jnp.where(hit, scores, 0.0), axis=0, keepdims=True))
        picked = jnp.where(hit, 1.0, picked)
        cand = jnp.where(hit, neg, cand)
    total = wts[0]
    for w in wts[1:]:
        total = total + w
    before = _dot(picked.astype(BF16), tri_ref[...]) + seen[...]
    ranks = [jnp.sum(jnp.where(row == ei, before, 0.0), axis=0, keepdims=True) for ei in eidx]
    seen[...] = seen[...] + jnp.sum(picked, axis=1, keepdims=True)
    eidx_ref[...] = jnp.concatenate(eidx, axis=0).astype(jnp.int32)
    wts_ref[...] = jnp.concatenate([w / total * ROUTED_SCALE for w in wts], axis=0)
    rank_ref[...] = jnp.concatenate(ranks, axis=0).astype(jnp.int32)
    cnt_ref[...] = seen[...].astype(jnp.int32)


def _router(h, router_w, router_bias, tm):
    t, d = h.shape
    n_e = router_w.shape[1]
    w_hi, w_lo = _split_hi_lo(router_w.T)
    i = np.arange(tm)
    tri = jnp.asarray((i[:, None] < i[None, :]).astype(np.float32), BF16)
    tok_spec = pl.BlockSpec((TOP_K, tm), lambda i: (0, i))
    return pl.pallas_call(
        _router_kernel,
        grid=(t // tm,),
        in_specs=[pl.BlockSpec((tm, d), lambda i: (i, 0)),
                  pl.BlockSpec((n_e, d), lambda i: (0, 0)),
                  pl.BlockSpec((n_e, d), lambda i: (0, 0)),
                  pl.BlockSpec((n_e, 1), lambda i: (0, 0)),
                  pl.BlockSpec((tm, tm), lambda i: (0, 0))],
        out_specs=[tok_spec, tok_spec, tok_spec, pl.BlockSpec((n_e, 1), lambda i: (0, 0))],
        out_shape=[jax.ShapeDtypeStruct((TOP_K, t), jnp.int32), jax.ShapeDtypeStruct((TOP_K, t), F32),
                   jax.ShapeDtypeStruct((TOP_K, t), jnp.int32), jax.ShapeDtypeStruct((n_e, 1), jnp.int32)],
        scratch_shapes=[pltpu.VMEM((n_e, 1), F32)],
        compiler_params=_cparams("arbitrary"),
        name="router",
    )(h, w_hi, w_lo, router_bias.reshape(n_e, 1).astype(F32), tri)


def _slot_kernel(eidx_ref, rank_ref, seg_ref, slot_ref):
    n_e = seg_ref.shape[0]
    tm = eidx_ref.shape[1]
    row = lax.broadcasted_iota(jnp.int32, (n_e, tm), 0)
    seg = seg_ref[...]
    slots = []
    for k in range(eidx_ref.shape[0]):
        start = jnp.sum(jnp.where(row == eidx_ref[k:k + 1, :], seg, 0.0), axis=0, keepdims=True)
        slots.append(start.astype(jnp.int32) + rank_ref[k:k + 1, :])
    slot_ref[...] = jnp.concatenate(slots, axis=0)


def _assignment_slots(eidx_t, rank_t, seg_start, tm):
    k, t = eidx_t.shape
    n_e = seg_start.shape[0]
    tok_spec = pl.BlockSpec((k, tm), lambda i: (0, i))
    return pl.pallas_call(
        _slot_kernel,
        grid=(t // tm,),
        in_specs=[tok_spec, tok_spec, pl.BlockSpec((n_e, 1), lambda i: (0, 0))],
        out_specs=tok_spec,
        out_shape=jax.ShapeDtypeStruct((k, t), jnp.int32),
        compiler_params=_cparams("arbitrary"),
        name="assignment_slots",
    )(eidx_t, rank_t, seg_start.reshape(n_e, 1).astype(F32))


def _expert_layout(counts, n_assign, unit_rows):
    n_e = counts.shape[0]
    padded = (counts + ROW_ALIGN - 1) // ROW_ALIGN * ROW_ALIGN
    seg_end = jnp.cumsum(padded)
    seg_start = seg_end - padded
    n_slots = n_assign + n_e * ROW_ALIGN
    units_per_e = (counts + unit_rows - 1) // unit_rows
    unit_end = jnp.cumsum(units_per_e)
    unit_first = unit_end - units_per_e
    n_units = unit_end[-1]
    max_units = n_e + n_assign // unit_rows
    uid = jnp.arange(max_units, dtype=jnp.int32)
    ue = jnp.minimum(jnp.sum(uid[:, None] >= unit_end[None, :], axis=1), n_e - 1).astype(jnp.int32)
    j = uid - unit_first[ue]
    valid = uid < n_units
    last_e = ue[jnp.maximum(n_units - 1, 0)]
    unit_e = jnp.where(valid, ue, last_e)
    unit_start = jnp.where(valid, seg_start[ue] + j * unit_rows, 0)
    unit_nrows = jnp.where(valid, jnp.clip(padded[ue] - j * unit_rows, 0, unit_rows), 0)
    n_units_tail = jnp.stack([n_units, seg_end[-1]]).astype(jnp.int32)
    pad_group = jnp.where(padded > 0, seg_end - ROW_ALIGN, -1).astype(jnp.int32)
    return (seg_start.astype(jnp.int32), pad_group, unit_e.astype(jnp.int32), unit_start.astype(jnp.int32),
            unit_nrows.astype(jnp.int32), n_units_tail, n_slots, max_units)


def _dispatch_kernel(slot_ref, pad_ref, tail_ref, x_ref, xs_hbm, zrows, zsem, sem, *, tm, k, t):
    i = pl.program_id(0)

    @pl.when(i == 0)
    def _():
        zrows[...] = jnp.zeros_like(zrows)
        tail0 = pl.multiple_of(tail_ref[1], ROW_ALIGN)
        tail_groups = (xs_hbm.shape[0] - tail0) // ROW_ALIGN
        n_e = pad_ref.shape[0]

        def zero_group(r0):
            return pltpu.make_async_copy(zrows, xs_hbm.at[pl.ds(pl.multiple_of(r0, ROW_ALIGN), ROW_ALIGN)], zsem)

        def pad_start(e, carry):
            @pl.when(pad_ref[e] >= 0)
            def _():
                zero_group(pad_ref[e]).start()
            return carry

        def pad_wait(e, carry):
            @pl.when(pad_ref[e] >= 0)
            def _():
                zero_group(pad_ref[e]).wait()
            return carry

        def tail_start(g, carry):
            zero_group(tail0 + g * ROW_ALIGN).start()
            return carry

        def tail_wait(g, carry):
            zero_group(tail0 + g * ROW_ALIGN).wait()
            return carry

        lax.fori_loop(0, n_e, pad_start, 0)
        lax.fori_loop(0, tail_groups, tail_start, 0)
        lax.fori_loop(0, n_e, pad_wait, 0)
        lax.fori_loop(0, tail_groups, tail_wait, 0)

    def issue(r, carry):
        for kk in range(k):
            s = slot_ref[kk * t + i * tm + r]
            pltpu.make_async_copy(x_ref.at[pl.ds(r, 1)], xs_hbm.at[pl.ds(s, 1)], sem).start()
        return carry

    lax.fori_loop(0, tm, issue, 0)
    for kk in range(k):
        pltpu.make_async_copy(x_ref, xs_hbm.at[pl.ds(0, tm)], sem).wait()


def _dispatch(xp, slot_flat, pad_group, n_units_tail, n_slots, tm, k):
    t, dh = xp.shape
    grid_spec = pltpu.PrefetchScalarGridSpec(
        num_scalar_prefetch=3,
        grid=(t // tm,),
        in_specs=[pl.BlockSpec((tm, dh), lambda i, *_: (i, 0))],
        out_specs=pl.BlockSpec(memory_space=pl.ANY),
        scratch_shapes=[pltpu.VMEM((ROW_ALIGN, dh), xp.dtype),
                        pltpu.SemaphoreType.DMA(()), pltpu.SemaphoreType.DMA(())])
    return pl.pallas_call(
        functools.partial(_dispatch_kernel, tm=tm, k=k, t=t),
        grid_spec=grid_spec,
        out_shape=jax.ShapeDtypeStruct((n_slots, dh), xp.dtype),
        compiler_params=_cparams("arbitrary"),
        name="moe_dispatch",
    )(slot_flat, pad_group, n_units_tail, xp)


def _moe_kernel(ue_ref, us_ref, ur_ref, nu_ref,
                xs_hbm, wg_ref, wu_ref, wd_ref, y_hbm,
                xbuf, xbf, hbuf, ybuf, zrows, wgb, wub, wdb, xsem, ysem, zsem, *, nf, nd):
    u = pl.program_id(0)
    s = pl.program_id(1)
    n_units = nu_ref[0]
    tn = ybuf.shape[2]

    def for_each(count, fn):
        def body(j, carry):
            fn(j)
            return carry
        lax.fori_loop(0, count, body, 0)

    def x_group(unit, g):
        r0 = pl.multiple_of(g * ROW_ALIGN, ROW_ALIGN)
        src0 = pl.multiple_of(us_ref[unit], ROW_ALIGN) + r0
        return pltpu.make_async_copy(xs_hbm.at[pl.ds(src0, ROW_ALIGN)],
                                     xbuf.at[unit % 2, pl.ds(r0, ROW_ALIGN)], xsem.at[unit % 2])

    def y_slot(unit, n):
        return (unit * nd + n) % 2

    def y_group(unit, n, g):
        r0 = pl.multiple_of(g * ROW_ALIGN, ROW_ALIGN)
        dst0 = pl.multiple_of(us_ref[unit], ROW_ALIGN) + r0
        return pltpu.make_async_copy(ybuf.at[y_slot(unit, n), pl.ds(r0, ROW_ALIGN)],
                                     y_hbm.at[pl.ds(dst0, ROW_ALIGN), pl.ds(pl.multiple_of(n * tn, tn), tn)],
                                     ysem.at[y_slot(unit, n)])

    def groups(unit):
        return ur_ref[unit] // ROW_ALIGN

    @pl.when(jnp.logical_and(u == 0, s == 0))
    def _():
        xbuf[...] = jnp.zeros_like(xbuf)
        zrows[...] = jnp.zeros_like(zrows)
        tail0 = pl.multiple_of(nu_ref[1], ROW_ALIGN)

        def tail_copy(g):
            r0 = pl.multiple_of(tail0 + g * ROW_ALIGN, ROW_ALIGN)
            return pltpu.make_async_copy(zrows, y_hbm.at[pl.ds(r0, ROW_ALIGN)], zsem)

        tail_groups = (y_hbm.shape[0] - tail0) // ROW_ALIGN
        for_each(tail_groups, lambda g: tail_copy(g).start())
        for_each(tail_groups, lambda g: tail_copy(g).wait())

    @pl.when(u < n_units)
    def _():
        def for_row_count(fn):
            fn(xbf.shape[0])

        @pl.when(s == 0)
        def _():
            @pl.when(u == 0)
            def _():
                for_each(groups(u), lambda g: x_group(u, g).start())

            for_each(groups(u), lambda g: x_group(u, g).wait())

            @pl.when(u + 1 < n_units)
            def _():
                for_each(groups(u + 1), lambda g: x_group(u + 1, g).start())

            def unpack(m):
                xbf[0:m, :] = _unpack_bf16_pairs(xbuf[u % 2, 0:m, :])

            for_row_count(unpack)

        @pl.when(s < nf)
        def _():
            wgb[...] = wg_ref[...].astype(BF16)
            wub[...] = wu_ref[...].astype(BF16)

            def up_proj(m):
                xb = xbf[0:m, :]
                g = _dot(xb, wgb[...])
                up = _dot(xb, wub[...])
                hbuf[s, 0:m, :] = (_silu(g) * up).astype(BF16)

            for_row_count(up_proj)

        @pl.when(s >= nf)
        def _():
            n = s - nf
            wdb[...] = wd_ref[...].astype(BF16)

            def down_proj(m):
                h = jnp.concatenate([hbuf[j, 0:m, :] for j in range(nf)], axis=1)
                ybuf[y_slot(u, n), 0:m, :] = _dot(h, wdb[...])

            for_row_count(down_proj)
            for_each(groups(u), lambda g: y_group(u, n, g).start())

            @pl.when(n > 0)
            def _():
                for_each(groups(u), lambda g: y_group(u, n - 1, g).wait())

            @pl.when(jnp.logical_and(n == 0, u > 0))
            def _():
                for_each(groups(u - 1), lambda g: y_group(u - 1, nd - 1, g).wait())

            @pl.when(jnp.logical_and(n == nd - 1, u == n_units - 1))
            def _():
                for_each(groups(u), lambda g: y_group(u, n, g).wait())


def _grouped_swiglu(xs, unit_e, unit_start, unit_nrows, n_units_tail, max_units,
                    w_gate, w_up, w_down, unit_rows, name):
    n_slots, dh = xs.shape
    d = 2 * dh
    n_e, _, de = w_gate.shape
    tf = _tile(de, MOE_F_TILE)
    nf = de // tf
    tn = _tile(d, MOE_OUT_TILE)
    nd = d // tn

    def up_map(u, s, ue, us, ur, nu):
        return ue[u], 0, jnp.where(u < nu[0], jnp.minimum(s, nf - 1), nf - 1)

    def down_map(u, s, ue, us, ur, nu):
        prev = jnp.maximum(u - 1, 0)
        in_up = jnp.logical_and(s < nf, u > 0)
        e = jnp.where(in_up, ue[prev], ue[u])
        n = jnp.where(in_up, nd - 1, jnp.maximum(s - nf, 0))
        return e, 0, jnp.where(u < nu[0], n, nd - 1)

    grid_spec = pltpu.PrefetchScalarGridSpec(
        num_scalar_prefetch=4,
        grid=(max_units, nf + nd),
        in_specs=[pl.BlockSpec(memory_space=pl.ANY),
                  pl.BlockSpec((None, d, tf), up_map),
                  pl.BlockSpec((None, d, tf), up_map),
                  pl.BlockSpec((None, de, tn), down_map)],
        out_specs=pl.BlockSpec(memory_space=pl.ANY),
        scratch_shapes=[pltpu.VMEM((2, unit_rows, dh), jnp.uint32),
                        pltpu.VMEM((unit_rows, d), BF16),
                        pltpu.VMEM((nf, unit_rows, tf), BF16),
                        pltpu.VMEM((2, unit_rows, tn), F32),
                        pltpu.VMEM((ROW_ALIGN, d), F32),
                        pltpu.VMEM((d, tf), BF16), pltpu.VMEM((d, tf), BF16), pltpu.VMEM((de, tn), BF16),
                        pltpu.SemaphoreType.DMA((2,)), pltpu.SemaphoreType.DMA((2,)),
                        pltpu.SemaphoreType.DMA(())])
    return pl.pallas_call(
        functools.partial(_moe_kernel, nf=nf, nd=nd),
        grid_spec=grid_spec,
        out_shape=jax.ShapeDtypeStruct((n_slots, d), F32),
        compiler_params=_cparams("arbitrary", "arbitrary"),
        name=name,
    )(unit_e, unit_start, unit_nrows, n_units_tail, xs, w_gate, w_up, w_down)


def _combine_kernel(pos_ref, ys_hbm, wts_ref, ysh_ref, x1_ref, mod_ref, fw_ref, out_ref, buf, sem, *,
                    tm, k, t, row0):
    i = pl.program_id(0)
    n = pl.num_programs(0)

    def gather(step):
        def issue(r, carry):
            for kk in range(k):
                p = pos_ref[kk * t + row0 + step * tm + r]
                pltpu.make_async_copy(ys_hbm.at[pl.ds(p, 1)], buf.at[step % 2, kk, pl.ds(r, 1)],
                                      sem.at[step % 2]).start()
            return carry

        lax.fori_loop(0, tm, issue, 0)

    @pl.when(i == 0)
    def _():
        gather(i)

    @pl.when(i + 1 < n)
    def _():
        gather(i + 1)

    for kk in range(k):
        pltpu.make_async_copy(ys_hbm.at[pl.ds(0, tm)], buf.at[i % 2, kk], sem.at[i % 2]).wait()
    acc = buf[i % 2, 0] * wts_ref[:, 0:1]
    for kk in range(1, k):
        acc = acc + buf[i % 2, kk] * wts_ref[:, kk:kk + 1]
    ff = acc + ysh_ref[...]
    x2 = x1_ref[...] + mod_ref[0, 5:6, :] * ff
    out_ref[...] = x2 * lax.rsqrt(jnp.mean(x2 * x2, axis=-1, keepdims=True) + EPS) * fw_ref[...]


def _combine(y_sorted, assign_slot, wts, y_shared, x1, mod3, final_w, tm, ctx_rows, lat_seq, first_row, n_rows):
    t, d = x1.shape
    k = wts.shape[1]
    cb = ctx_rows // tm
    bps = lat_seq // tm
    ctx_row = (t - ctx_rows) // lat_seq
    b0 = first_row // tm
    mod_map = _mod_row_map(cb, bps, ctx_row)
    grid_spec = pltpu.PrefetchScalarGridSpec(
        num_scalar_prefetch=1,
        grid=(n_rows // tm,),
        in_specs=[pl.BlockSpec(memory_space=pl.ANY),
                  pl.BlockSpec((tm, k), lambda i, pos: (i + b0, 0)),
                  pl.BlockSpec((tm, d), lambda i, pos: (i + b0, 0)),
                  pl.BlockSpec((tm, d), lambda i, pos: (i + b0, 0)),
                  pl.BlockSpec((1, 6, d), lambda i, pos: mod_map(i + b0)),
                  pl.BlockSpec((1, d), lambda i, pos: (0, 0))],
        out_specs=pl.BlockSpec((tm, d), lambda i, pos: (i, 0)),
        scratch_shapes=[pltpu.VMEM((2, k, tm, d), F32), pltpu.SemaphoreType.DMA((2,))])
    return pl.pallas_call(
        functools.partial(_combine_kernel, tm=tm, k=k, t=t, row0=first_row),
        grid_spec=grid_spec,
        out_shape=jax.ShapeDtypeStruct((n_rows, d), F32),
        compiler_params=_cparams("arbitrary"),
        name="moe_combine",
    )(assign_slot, y_sorted, wts, y_shared, x1, mod3, final_w)


def _grid_pos_embed(n_tokens, d_model):
    rows = n_tokens // GRID_W
    r, col = jnp.meshgrid(jnp.arange(rows, dtype=F32), jnp.arange(GRID_W, dtype=F32), indexing="ij")
    n_freq = d_model // 4
    omega = 1.0 / (10000.0 ** (jnp.arange(n_freq, dtype=F32) / n_freq))

    def enc(p):
        ang = p.reshape(-1)[:, None] * omega[None, :]
        return jnp.concatenate([jnp.sin(ang), jnp.cos(ang)], axis=-1)

    return jnp.concatenate([enc(r), enc(col)], axis=-1)


def _pad_rows(a, rows, row0=0):
    return jnp.zeros((rows,) + a.shape[1:], a.dtype).at[row0:row0 + a.shape[0]].set(a)


def kernel(x_prompt, x_sample, state_hgrn_fwd, state_hgrn_bwd, state_gla_fwd, state_gla_bwd, c, c_ctx, norm1_w, norm2_w, ada_w, ada_b, w_in, hg_lb_logits, gla_gk_up_fwd, gla_gk_bias_fwd, gla_gk_up_bwd, gla_gk_bias_bwd, hg_norm_w, gla_norm_w, w_out, router_w, router_bias, exp_w_gate, exp_w_up, exp_w_down, sh_w_gate, sh_w_up, sh_w_down, final_norm_w):
    n_ctx, ctx_seq, d = x_prompt.shape
    n_lat, lat_seq, _ = x_sample.shape
    depth = norm1_w.shape[0]
    assert depth == 1, "one trunk layer"
    _, _, hg_heads, hg_dk, hg_dv = state_hgrn_fwd.shape
    _, _, gla_heads, gla_dk, gla_dv = state_gla_fwd.shape
    assert hg_dk == hg_dv
    rank = gla_gk_up_fwd.shape[1]
    n_experts = exp_w_gate.shape[1]
    hg_w = hg_heads * hg_dk
    gla_kw = gla_heads * gla_dk
    gla_w = gla_heads * gla_dv
    tp, ts = n_ctx * ctx_seq, n_lat * lat_seq
    t = tp + ts
    assert tp % lat_seq == 0 and 2 * rank <= LANES and n_lat < 8
    tm = min(256, ctx_seq)
    layer = 0

    lb_all = jnp.cumsum(jax.nn.softmax(hg_lb_logits.astype(F32), axis=1), axis=1)
    pos = _grid_pos_embed(lat_seq, d)

    c_rows = jnp.zeros((8, d), F32).at[:n_lat].set(c).at[n_lat].set(c_ctx)
    mod = _ada(c_rows, ada_w[layer], ada_b[layer].reshape(1, -1))
    mod3 = mod.reshape(8, 6, d)

    x0, h1 = _norm1(x_prompt.reshape(tp, d), x_sample.reshape(ts, d), pos, mod3,
                    norm1_w[layer].reshape(1, d), tm, lat_seq)

    lr0 = 5 * hg_w + 2 * gla_kw + gla_w
    w_l = w_in[layer]
    w_main = jnp.concatenate([w_l[:, :lr0], w_l[:, lr0 + 2 * rank:]], axis=1).astype(BF16)
    w_lr = jnp.zeros((d, LANES), BF16).at[:, :2 * rank].set(w_l[:, lr0:lr0 + 2 * rank].astype(BF16))
    proj = _matmul(h1, w_main, 1024, 1024, "in_proj")
    lr = _matmul(h1, w_lr, 1024, LANES, "in_proj_lr")

    consts = {c: _level_constants(c) for c in {min(HGRN_CTX_CHUNK, ctx_seq), min(GLA_CTX_CHUNK, ctx_seq),
                                               min(HGRN_LAT_CHUNK, lat_seq), min(GLA_LAT_CHUNK, lat_seq)}}
    up_f = _pad_rows(gla_gk_up_fwd[layer], LANES, 0)
    up_b = _pad_rows(gla_gk_up_bwd[layer], LANES, rank)
    bias_f = gla_gk_bias_fwd[layer].reshape(1, gla_kw)
    bias_b = gla_gk_bias_bwd[layer].reshape(1, gla_kw)
    hg_args = dict(heads=hg_heads, dk=hg_dk)
    gla_args = dict(heads=gla_heads, dk=gla_dk, dv=gla_dv, col0=5 * hg_w)

    hg_ctx, new_hf, new_hb = _hgrn(proj, lb_all[0, layer], lb_all[1, layer], hg_norm_w[layer], None, None,
                                   consts[min(HGRN_CTX_CHUNK, ctx_seq)], batch=n_ctx, seq=ctx_seq, row_block0=0, **hg_args)
    hg_lat, _, _ = _hgrn(proj, lb_all[0, layer], lb_all[1, layer], hg_norm_w[layer],
                         state_hgrn_fwd[:, layer], state_hgrn_bwd[:, layer], consts[min(HGRN_LAT_CHUNK, lat_seq)],
                         batch=n_lat, seq=lat_seq, row_block0=tp // lat_seq, **hg_args)
    gla_ctx, new_gf, new_gb = _gla(proj, lr, up_f, up_b, bias_f, bias_b, gla_norm_w[layer], None, None,
                                   consts[min(GLA_CTX_CHUNK, ctx_seq)], batch=n_ctx, seq=ctx_seq, row_block0=0, **gla_args)
    gla_lat, _, _ = _gla(proj, lr, up_f, up_b, bias_f, bias_b, gla_norm_w[layer],
                         state_gla_fwd[:, layer], state_gla_bwd[:, layer], consts[min(GLA_LAT_CHUNK, lat_seq)],
                         batch=n_lat, seq=lat_seq, row_block0=tp // lat_seq, **gla_args)

    w_o = w_out[layer].astype(BF16)
    mix_ctx = _matmul2(hg_ctx, w_o[:hg_w], gla_ctx, w_o[hg_w:], 1024, 1024, "out_proj_ctx")
    mix_lat = _matmul2(hg_lat, w_o[:hg_w], gla_lat, w_o[hg_w:], 1024, 1024, "out_proj_lat")

    x1, h2, h2p = _norm2(x0, mix_ctx, mix_lat, mod3, norm2_w[layer].reshape(1, d), tm, lat_seq)

    tok_tile = _tile(t, 512)
    eidx_t, wts_t, rank_t, counts = _router(h2, router_w[layer], router_bias[layer], tok_tile)
    seg_start, pad_group, unit_e, unit_start, unit_nrows, n_units_tail, n_slots, max_units = _expert_layout(
        counts.reshape(n_experts), t * TOP_K, MOE_UNIT_ROWS)
    slot_flat = _assignment_slots(eidx_t, rank_t, seg_start, tok_tile).reshape(TOP_K * t)
    xs = _dispatch(h2p, slot_flat, pad_group, n_units_tail, n_slots, tm, TOP_K)
    y_sorted = _grouped_swiglu(xs, unit_e, unit_start, unit_nrows, n_units_tail, max_units,
                               exp_w_gate[layer], exp_w_up[layer], exp_w_down[layer], MOE_UNIT_ROWS, "moe_experts")

    assert t % MOE_UNIT_ROWS == 0
    sh_units = t // MOE_UNIT_ROWS
    sh_uid = jnp.arange(sh_units, dtype=jnp.int32)
    y_shared = _grouped_swiglu(h2p, jnp.zeros((sh_units,), jnp.int32), sh_uid * MOE_UNIT_ROWS,
                               jnp.full((sh_units,), MOE_UNIT_ROWS, jnp.int32),
                               jnp.array([sh_units, t], jnp.int32), sh_units,
                               sh_w_gate[layer][None], sh_w_up[layer][None], sh_w_down[layer][None],
                               MOE_UNIT_ROWS, "shared_expert")

    fin = (y_sorted, slot_flat, wts_t.T, y_shared, x1, mod3, final_norm_w.reshape(1, d), 64, tp, lat_seq)
    y_prompt = _combine(*fin, 0, tp).reshape(n_ctx, ctx_seq, d)
    y_sample = _combine(*fin, tp, ts).reshape(n_lat, lat_seq, d)
    return (y_prompt, y_sample, new_hf[:, None], new_hb[:, None], new_gf[:, None], new_gb[:, None])
```
